```python
import math
import jax, jax.numpy as jnp
from jax import lax
import numpy as np

D_MODEL = 1024
BATCH = 8
SEQ = 4096
DEPTH = 2
DEC_BATCH = 32
DEC_SEQ = 8
PAST_LEN = 16384
PAGE_SIZE = 128

N_MIXERS = 2
N_LRU_LAYERS = (DEPTH + 1) // 2
N_ATTN_LAYERS = DEPTH // 2
D_RNN = D_MODEL
LRU_BLOCKS = 4
LRU_BLOCK_W = D_RNN // LRU_BLOCKS
CONV_W = 4
LRU_C = 8.0
N_HEADS = 8
HEAD_DIM = D_MODEL // (2 * N_HEADS)
QK_DIM = 2 * HEAD_DIM
V_DIM = 2 * HEAD_DIM
Q_BLOCK = 128
D_FF = 2816
D_PLE = 256
NORM_EPS = 1e-6
NEG_INF = -1e30

kernel_name = "hybrid_rglru_diffattn_decoder_step"


def rmsnorm(x, g):
    xf = x.astype(jnp.float32)
    y = xf * lax.rsqrt(jnp.mean(xf * xf, axis=-1, keepdims=True) + NORM_EPS)
    return (y * g.astype(jnp.float32)).astype(x.dtype)


def swiglu(x, w_in, w_out):
    g, u = jnp.split(x @ w_in, 2, axis=-1)
    return (jax.nn.silu(g) * u) @ w_out


def half_ffn(x, g, w_in, w_out):
    return x + 0.5 * swiglu(rmsnorm(x, g), w_in, w_out)


def ple_add(x, p, g, w_gate, w_proj):
    gate = jax.nn.sigmoid(rmsnorm(x, g) @ w_gate)
    return x + gate * (p @ w_proj)


def rglru_mixer(x, conv_buf, h0, w_in, conv_w, conv_b, ga_w, ga_b, gx_w, gx_b, lam, w_out):
    B, T, _ = x.shape
    y_br, u = jnp.split(x @ w_in, 2, axis=-1)
    y_br = jax.nn.gelu(y_br)
    full = jnp.concatenate([conv_buf.astype(u.dtype), u], axis=1)
    conv = conv_b + full[:, 0:T] * conv_w[0]
    for k in range(1, CONV_W):
        conv = conv + full[:, k:k + T] * conv_w[k]
    new_buf = full[:, T:]
    xb = conv.reshape(B, T, LRU_BLOCKS, LRU_BLOCK_W)
    r = jax.nn.sigmoid(jnp.einsum('btnc,ncd->btnd', xb, ga_w).reshape(B, T, D_RNN) + ga_b)
    i = jax.nn.sigmoid(jnp.einsum('btnc,ncd->btnd', xb, gx_w).reshape(B, T, D_RNN) + gx_b)
    log_a = -LRU_C * r.astype(jnp.float32) * jax.nn.softplus(-lam.astype(jnp.float32))
    a = jnp.exp(log_a)
    mult = jnp.sqrt(-jnp.expm1(2.0 * log_a))
    b = mult * (i * conv).astype(jnp.float32)
    b = b.at[:, 0].add(a[:, 0] * h0.astype(jnp.float32))

    def combine(lhs, rhs):
        a1, b1 = lhs
        a2, b2 = rhs
        return a1 * a2, a2 * b1 + b2

    _, h = lax.associative_scan(combine, (a, b), axis=1)
    out = (y_br * h.astype(x.dtype)) @ w_out
    return out, new_buf, h[:, -1].astype(x.dtype)


def alibi_slopes():
    return jnp.asarray(2.0 ** (-8.0 * (np.arange(N_HEADS) + 1) / N_HEADS), dtype=jnp.float32)


def alibi_bias(q_pos, k_pos, slopes):
    dist = (q_pos[:, None] - k_pos[None, :]).astype(jnp.float32)
    return jnp.where(dist >= 0, -slopes[:, None, None] * dist, NEG_INF)


def diff_lambda(lq1, lk1, lq2, lk2, lam_init):
    f = jnp.float32
    return (jnp.exp(jnp.sum(lq1.astype(f) * lk1.astype(f)))
            - jnp.exp(jnp.sum(lq2.astype(f) * lk2.astype(f))) + lam_init)


def diff_core(q, k, v, bias, lam):
    q1, q2 = jnp.split(q, 2, axis=-1)
    k1, k2 = jnp.split(k, 2, axis=-1)
    scale = HEAD_DIM ** -0.5
    s1 = jnp.einsum('bqhe,bkhe->bhqk', q1, k1, preferred_element_type=jnp.float32) * scale + bias
    s2 = jnp.einsum('bqhe,bkhe->bhqk', q2, k2, preferred_element_type=jnp.float32) * scale + bias
    p = jax.nn.softmax(s1, axis=-1) - lam * jax.nn.softmax(s2, axis=-1)
    return jnp.einsum('bhqk,bkhd->bqhd', p.astype(v.dtype), v, preferred_element_type=jnp.float32)


def diff_head_out(o, subln_g, lam_init, w_o, dtype):
    B, T = o.shape[:2]
    o = rmsnorm(o, subln_g) * (1.0 - lam_init)
    return o.reshape(B, T, N_HEADS * V_DIM).astype(dtype) @ w_o


def qkv_split(x, w_qkv):
    B, T, _ = x.shape
    qkv = x @ w_qkv
    q = qkv[..., :N_HEADS * QK_DIM].reshape(B, T, N_HEADS, QK_DIM)
    k = qkv[..., N_HEADS * QK_DIM:2 * N_HEADS * QK_DIM].reshape(B, T, N_HEADS, QK_DIM)
    v = qkv[..., 2 * N_HEADS * QK_DIM:].reshape(B, T, N_HEADS, V_DIM)
    return q, k, v


def diff_attn_prompt(x, w_qkv, lq1, lk1, lq2, lk2, subln_g, w_o, lam_init):
    B, T, _ = x.shape
    q, k, v = qkv_split(x, w_qkv)
    lam = diff_lambda(lq1, lk1, lq2, lk2, lam_init)
    slopes = alibi_slopes()
    k_pos = jnp.arange(T, dtype=jnp.int32)
    n_qb = T // Q_BLOCK
    qb = q.reshape(B, n_qb, Q_BLOCK, N_HEADS, QK_DIM).transpose(1, 0, 2, 3, 4)

    def block(args):
        idx, q_blk = args
        q_pos = idx * Q_BLOCK + jnp.arange(Q_BLOCK, dtype=jnp.int32)
        return diff_core(q_blk, k, v, alibi_bias(q_pos, k_pos, slopes), lam)

    o = lax.map(block, (jnp.arange(n_qb, dtype=jnp.int32), qb))
    o = o.transpose(1, 0, 2, 3, 4).reshape(B, T, N_HEADS, V_DIM)
    return diff_head_out(o, subln_g, lam_init, w_o, x.dtype), k, v


def diff_attn_sample(x, cache_k, cache_v, layer, page_table, w_qkv, lq1, lk1, lq2, lk2,
                     subln_g, w_o, lam_init):
    B, T, _ = x.shape
    q, k, v = qkv_split(x, w_qkv)
    lam = diff_lambda(lq1, lk1, lq2, lk2, lam_init)
    past = page_table.shape[1] * PAGE_SIZE
    q_pos = past + jnp.arange(T, dtype=jnp.int32)
    k_pos = jnp.arange(past + T, dtype=jnp.int32)
    bias = alibi_bias(q_pos, k_pos, alibi_slopes())

    def one_seq(args):
        q_b, k_b, v_b, pages = args
        k_past = cache_k[layer, pages].reshape(past, N_HEADS, QK_DIM).astype(k_b.dtype)
        v_past = cache_v[layer, pages].reshape(past, N_HEADS, V_DIM).astype(v_b.dtype)
        k_all = jnp.concatenate([k_past, k_b], axis=0)
        v_all = jnp.concatenate([v_past, v_b], axis=0)
        return diff_core(q_b[None], k_all[None], v_all[None], bias, lam)[0]

    o = lax.map(one_seq, (q, k, v, page_table))
    return diff_head_out(o, subln_g, lam_init, w_o, x.dtype), k, v


def setup_inputs(seed: int = 0) -> dict:
    key = jax.random.key(seed)
    ks = jax.random.split(key, 40)
    f = jnp.float32
    nrm = lambda k, shape, s: jax.random.normal(k, shape, f) * s
    n_pages = PAST_LEN // PAGE_SIZE
    n_used = DEC_BATCH * n_pages
    n_phys = n_used + max(1, n_used // 4)
    page_table = jax.random.permutation(ks[0], n_phys)[:n_used].reshape(DEC_BATCH, n_pages).astype(jnp.int32)
    a_lo, a_hi = 0.9 ** (1.0 / LRU_C), 0.999 ** (1.0 / LRU_C)
    a0 = jax.random.uniform(ks[1], (N_LRU_LAYERS, D_RNN), f, a_lo, a_hi)
    lru_lambda = jnp.log(a0) - jnp.log1p(-a0)
    return {
        "x_prompt": nrm(ks[2], (BATCH, SEQ, D_MODEL), 1.0),
        "x_sample": nrm(ks[3], (DEC_BATCH, DEC_SEQ, D_MODEL), 1.0),
        "cache_k": nrm(ks[4], (N_ATTN_LAYERS, n_phys, PAGE_SIZE, N_HEADS, QK_DIM), 1.0),
        "cache_v": nrm(ks[5], (N_ATTN_LAYERS, n_phys, PAGE_SIZE, N_HEADS, V_DIM), 1.0),
        "page_table": page_table,
        "state_conv": nrm(ks[6], (N_LRU_LAYERS, DEC_BATCH, CONV_W - 1, D_RNN), 1.0),
        "state_h": nrm(ks[7], (N_LRU_LAYERS, DEC_BATCH, D_RNN), 0.5),
        "p_prompt": nrm(ks[8], (DEPTH, BATCH, SEQ, D_PLE), 1.0),
        "p_sample": nrm(ks[9], (DEPTH, DEC_BATCH, DEC_SEQ, D_PLE), 1.0),
        "norm_g": 1.0 + nrm(ks[10], (DEPTH, 4, D_MODEL), 0.05),
        "final_norm_g": 1.0 + nrm(ks[11], (D_MODEL,), 0.05),
        "ffn1_w_in": nrm(ks[12], (DEPTH, D_MODEL, 2 * D_FF), D_MODEL ** -0.5),
        "ffn1_w_out": nrm(ks[13], (DEPTH, D_FF, D_MODEL), D_FF ** -0.5),
        "ffn2_w_in": nrm(ks[14], (DEPTH, D_MODEL, 2 * D_FF), D_MODEL ** -0.5),
        "ffn2_w_out": nrm(ks[15], (DEPTH, D_FF, D_MODEL), D_FF ** -0.5),
        "ple_w_gate": nrm(ks[16], (DEPTH, D_MODEL, D_MODEL), D_MODEL ** -0.5),
        "ple_w_proj": nrm(ks[17], (DEPTH, D_PLE, D_MODEL), 0.5 * D_PLE ** -0.5),
        "lru_w_in": nrm(ks[18], (N_LRU_LAYERS, D_MODEL, 2 * D_RNN), D_MODEL ** -0.5),
        "lru_conv_w": nrm(ks[19], (N_LRU_LAYERS, CONV_W, D_RNN), CONV_W ** -0.5),
        "lru_conv_b": nrm(ks[20], (N_LRU_LAYERS, D_RNN), 0.01),
        "lru_gate_a_w": nrm(ks[21], (N_LRU_LAYERS, LRU_BLOCKS, LRU_BLOCK_W, LRU_BLOCK_W), LRU_BLOCK_W ** -0.5),
        "lru_gate_a_b": nrm(ks[22], (N_LRU_LAYERS, D_RNN), 0.01),
        "lru_gate_x_w": nrm(ks[23], (N_LRU_LAYERS, LRU_BLOCKS, LRU_BLOCK_W, LRU_BLOCK_W), LRU_BLOCK_W ** -0.5),
        "lru_gate_x_b": nrm(ks[24], (N_LRU_LAYERS, D_RNN), 0.01),
        "lru_lambda": lru_lambda,
        "lru_w_out": nrm(ks[25], (N_LRU_LAYERS, D_RNN, D_MODEL), D_RNN ** -0.5),
        "attn_w_qkv": nrm(ks[26], (N_ATTN_LAYERS, D_MODEL, N_HEADS * (2 * QK_DIM + V_DIM)), D_MODEL ** -0.5),
        "attn_lambda_q1": nrm(ks[27], (N_ATTN_LAYERS, HEAD_DIM), 0.1),
        "attn_lambda_k1": nrm(ks[28], (N_ATTN_LAYERS, HEAD_DIM), 0.1),
        "attn_lambda_q2": nrm(ks[29], (N_ATTN_LAYERS, HEAD_DIM), 0.1),
        "attn_lambda_k2": nrm(ks[30], (N_ATTN_LAYERS, HEAD_DIM), 0.1),
        "attn_subln_g": 1.0 + nrm(ks[31], (N_ATTN_LAYERS, V_DIM), 0.05),
        "attn_w_o": nrm(ks[32], (N_ATTN_LAYERS, N_HEADS * V_DIM, D_MODEL), (N_HEADS * V_DIM) ** -0.5),
    }


def reference(x_prompt, x_sample, cache_k, cache_v, page_table, state_conv, state_h,
              p_prompt, p_sample, norm_g, final_norm_g, ffn1_w_in, ffn1_w_out, ffn2_w_in,
              ffn2_w_out, ple_w_gate, ple_w_proj, lru_w_in, lru_conv_w, lru_conv_b,
              lru_gate_a_w, lru_gate_a_b, lru_gate_x_w, lru_gate_x_b, lru_lambda, lru_w_out,
              attn_w_qkv, attn_lambda_q1, attn_lambda_k1, attn_lambda_q2, attn_lambda_k2,
              attn_subln_g, attn_w_o):
    xp, xs = x_prompt, x_sample
    Bp = xp.shape[0]
    conv_zero = jnp.zeros((Bp, CONV_W - 1, D_RNN), xp.dtype)
    h_zero = jnp.zeros((Bp, D_RNN), xp.dtype)
    kp_l, vp_l, ks_l, vs_l = [], [], [], []
    cp_l, hp_l, cs_l, hs_l = [], [], [], []
    for i in range(DEPTH):
        j = i // N_MIXERS
        xp = half_ffn(xp, norm_g[i, 0], ffn1_w_in[i], ffn1_w_out[i])
        xs = half_ffn(xs, norm_g[i, 0], ffn1_w_in[i], ffn1_w_out[i])
        if i % N_MIXERS == 0:
            lru_w = (lru_w_in[j], lru_conv_w[j], lru_conv_b[j], lru_gate_a_w[j], lru_gate_a_b[j],
                     lru_gate_x_w[j], lru_gate_x_b[j], lru_lambda[j], lru_w_out[j])
            out_p, buf_p, h_p = rglru_mixer(rmsnorm(xp, norm_g[i, 1]), conv_zero, h_zero, *lru_w)
            out_s, buf_s, h_s = rglru_mixer(rmsnorm(xs, norm_g[i, 1]), state_conv[j], state_h[j], *lru_w)
            xp = xp + out_p
            xs = xs + out_s
            cp_l.append(buf_p); hp_l.append(h_p); cs_l.append(buf_s); hs_l.append(h_s)
        else:
            lam_init = 0.8 - 0.6 * math.exp(-0.3 * i)
            aw = (attn_w_qkv[j], attn_lambda_q1[j], attn_lambda_k1[j], attn_lambda_q2[j],
                  attn_lambda_k2[j], attn_subln_g[j], attn_w_o[j], lam_init)
            out_p, k_p, v_p = diff_attn_prompt(rmsnorm(xp, norm_g[i, 1]), *aw)
            out_s, k_s, v_s = diff_attn_sample(rmsnorm(xs, norm_g[i, 1]), cache_k, cache_v, j,
                                               page_table, *aw)
            xp = xp + out_p
            xs = xs + out_s
            kp_l.append(k_p); vp_l.append(v_p); ks_l.append(k_s); vs_l.append(v_s)
        xp = half_ffn(xp, norm_g[i, 2], ffn2_w_in[i], ffn2_w_out[i])
        xs = half_ffn(xs, norm_g[i, 2], ffn2_w_in[i], ffn2_w_out[i])
        xp = ple_add(xp, p_prompt[i], norm_g[i, 3], ple_w_gate[i], ple_w_proj[i])
        xs = ple_add(xs, p_sample[i], norm_g[i, 3], ple_w_gate[i], ple_w_proj[i])
    y_prompt = rmsnorm(xp, final_norm_g)
    y_sample = rmsnorm(xs, final_norm_g)
    k_prompt = jnp.stack(kp_l)
    v_prompt = jnp.stack(vp_l)
    k_sample = jnp.stack(ks_l)
    v_sample = jnp.stack(vs_l)
    conv_prompt = jnp.stack(cp_l)
    h_prompt = jnp.stack(hp_l)
    conv_sample = jnp.stack(cs_l)
    h_sample = jnp.stack(hs_l)
    return (y_prompt, y_sample, k_prompt, v_prompt, k_sample, v_sample,
            conv_prompt, h_prompt, conv_sample, h_sample)
```

```python
import functools
import math

import jax
import jax.numpy as jnp
from jax import lax
from jax.experimental import pallas as pl
from jax.experimental.pallas import tpu as pltpu

F32 = jnp.float32
BF16 = jnp.bfloat16

NORM_EPS = 1e-6
NEG_INF = -1e30
LRU_C = 8.0
CONV_W = 4
LRU_BLOCKS = 4
N_HEADS = 8
HEAD_DIM = 64
V_DIM = 128
PAGE_SIZE = 128
LANES = 128
SUBLANES = 8
VMEM_LIMIT_BYTES = 56 * 1024 * 1024

_NT = (((1,), (1,)), ((), ()))


def _params(*sem):
    return pltpu.CompilerParams(dimension_semantics=sem, vmem_limit_bytes=VMEM_LIMIT_BYTES)


def _const_spec(shape):
    nd = len(shape)
    return pl.BlockSpec(shape, lambda *_: (0,) * nd, pipeline_mode=pl.Buffered(1))


def _rms(x, g):
    y = x * lax.rsqrt(jnp.mean(x * x, axis=-1, keepdims=True) + NORM_EPS)
    return y * g


def _dot(a, b):
    return jnp.dot(a, b, preferred_element_type=F32)


def _ffn_kernel(x_ref, g_ref, win_ref, wout_ref, o_ref, *, d_ff, chunk):
    x = x_ref[...]
    xn = _rms(x, g_ref[...]).astype(BF16)
    acc = jnp.zeros(x.shape, F32)
    for c in range(d_ff // chunk):
        lo = c * chunk
        gt = _dot(xn, win_ref[:, lo:lo + chunk])
        ut = _dot(xn, win_ref[:, d_ff + lo:d_ff + lo + chunk])
        act = (gt * jax.nn.sigmoid(gt) * ut).astype(BF16)
        acc = acc + _dot(act, wout_ref[lo:lo + chunk, :])
    o_ref[...] = x + 0.5 * acc


def _ffn(x, g, w_in, w_out, tm):
    n, d = x.shape
    d_ff = w_out.shape[0]
    return pl.pallas_call(
        functools.partial(_ffn_kernel, d_ff=d_ff, chunk=256),
        out_shape=jax.ShapeDtypeStruct((n, d), F32),
        grid=(n // tm,),
        in_specs=[
            pl.BlockSpec((tm, d), lambda i: (i, 0)),
            _const_spec((1, d)),
            _const_spec(w_in.shape),
            _const_spec(w_out.shape),
        ],
        out_specs=pl.BlockSpec((tm, d), lambda i: (i, 0)),
        compiler_params=_params("parallel"),
        name="ffn",
    )(x, g.reshape(1, d), w_in, w_out)


def _ple_kernel(x_ref, p_ref, g_ref, wg_ref, wp_ref, fg_ref, o_ref, *, final_norm):
    x = x_ref[...]
    xn = _rms(x, g_ref[...]).astype(BF16)
    gate = jax.nn.sigmoid(_dot(xn, wg_ref[...]))
    proj = _dot(p_ref[...].astype(BF16), wp_ref[...])
    y = x + gate * proj
    if final_norm:
        y = _rms(y, fg_ref[...])
    o_ref[...] = y


def _ple(x, p, g, w_gate, w_proj, final_g, final_norm, tm):
    n, d = x.shape
    dp = p.shape[1]
    return pl.pallas_call(
        functools.partial(_ple_kernel, final_norm=final_norm),
        out_shape=jax.ShapeDtypeStruct((n, d), F32),
        grid=(n // tm,),
        in_specs=[
            pl.BlockSpec((tm, d), lambda i: (i, 0)),
            pl.BlockSpec((tm, dp), lambda i: (i, 0)),
            _const_spec((1, d)),
            _const_spec(w_gate.shape),
            _const_spec(w_proj.shape),
            _const_spec((1, d)),
        ],
        out_specs=pl.BlockSpec((tm, d), lambda i: (i, 0)),
        compiler_params=_params("parallel"),
        name="ple",
    )(x, p, g.reshape(1, d), w_gate, w_proj, final_g.reshape(1, d))


def _softplus(z):
    return jnp.maximum(z, 0.0) + jnp.log1p(jnp.exp(-jnp.abs(z)))


def _lru_kernel(x_ref, conv0_ref, h0_ref, g_ref, win_ref, cw_ref, cb_ref, gaw_ref, gab_ref,
                gxw_ref, gxb_ref, lam_ref, wout_ref,
                o_ref, buf_ref, hlast_ref,
                ubuf_ref, a_ref, b_ref, hs_ref, hcar_ref, *, tt, d_rnn):
    t = pl.program_id(1)
    tail = CONV_W - 1
    base = SUBLANES

    @pl.when(t == 0)
    def _():
        ubuf_ref[base - tail:base, :] = conv0_ref[...]
        hcar_ref[0:1, :] = h0_ref[...]

    x = x_ref[...]
    xn = _rms(x, g_ref[...]).astype(BF16)
    yu = _dot(xn, win_ref[...])
    y = jax.nn.gelu(yu[:, :d_rnn])
    u = yu[:, d_rnn:]
    ubuf_ref[base:base + tt, :] = u

    conv = cb_ref[...] + ubuf_ref[base - 3:base - 3 + tt, :] * cw_ref[0:1, :]
    conv = conv + ubuf_ref[base - 2:base - 2 + tt, :] * cw_ref[1:2, :]
    conv = conv + ubuf_ref[base - 1:base - 1 + tt, :] * cw_ref[2:3, :]
    conv = conv + u * cw_ref[3:4, :]

    bw = d_rnn // LRU_BLOCKS
    sp = _softplus(-lam_ref[...])
    for n in range(LRU_BLOCKS):
        sl = slice(n * bw, (n + 1) * bw)
        cn = conv[:, sl]
        cnb = cn.astype(BF16)
        r = jax.nn.sigmoid(_dot(cnb, gaw_ref[n]) + gab_ref[:, sl])
        i = jax.nn.sigmoid(_dot(cnb, gxw_ref[n]) + gxb_ref[:, sl])
        log_a = -LRU_C * r * sp[:, sl]
        a = jnp.exp(log_a)
        a_ref[:, sl] = a
        b_ref[:, sl] = jnp.sqrt(1.0 - a * a) * (i * cn)

    def group(gi, h):
        r0 = pl.multiple_of(gi * SUBLANES, SUBLANES)
        for j in range(SUBLANES):
            h = a_ref[pl.ds(r0 + j, 1), :] * h + b_ref[pl.ds(r0 + j, 1), :]
            hs_ref[pl.ds(r0 + j, 1), :] = h
        return h

    h = lax.fori_loop(0, tt // SUBLANES, group, hcar_ref[0:1, :])
    hcar_ref[0:1, :] = h

    z = (y * hs_ref[...]).astype(BF16)
    o_ref[...] = x + _dot(z, wout_ref[...])

    new_tail = ubuf_ref[base + tt - tail:base + tt, :]
    ubuf_ref[base - tail:base, :] = new_tail

    @pl.when(t == pl.num_programs(1) - 1)
    def _():
        buf_ref[...] = new_tail
        hlast_ref[...] = h


def _lru(x, conv0, h0, g, w_in, cw, cb, gaw, gab, gxw, gxb, lam, w_out, tt):
    b, t, d = x.shape
    d_rnn = w_out.shape[0]
    tail = CONV_W - 1
    row = lambda v: v.reshape(1, -1)
    out, buf, hlast = pl.pallas_call(
        functools.partial(_lru_kernel, tt=tt, d_rnn=d_rnn),
        out_shape=(
            jax.ShapeDtypeStruct((b, t, d), F32),
            jax.ShapeDtypeStruct((b, tail, d_rnn), F32),
            jax.ShapeDtypeStruct((b, 1, d_rnn), F32),
        ),
        grid=(b, t // tt),
        in_specs=[
            pl.BlockSpec((None, tt, d), lambda i, j: (i, j, 0)),
            pl.BlockSpec((None, tail, d_rnn), lambda i, j: (i, 0, 0)),
            pl.BlockSpec((None, 1, d_rnn), lambda i, j: (i, 0, 0)),
            _const_spec((1, d)),
            _const_spec(w_in.shape),
            _const_spec(cw.shape),
            _const_spec((1, d_rnn)),
            _const_spec(gaw.shape),
            _const_spec((1, d_rnn)),
            _const_spec(gxw.shape),
            _const_spec((1, d_rnn)),
            _const_spec((1, d_rnn)),
            _const_spec(w_out.shape),
        ],
        out_specs=(
            pl.BlockSpec((None, tt, d), lambda i, j: (i, j, 0)),
            pl.BlockSpec((None, tail, d_rnn), lambda i, j: (i, 0, 0)),
            pl.BlockSpec((None, 1, d_rnn), lambda i, j: (i, 0, 0)),
        ),
        scratch_shapes=[
            pltpu.VMEM((tt + 2 * SUBLANES, d_rnn), F32),
            pltpu.VMEM((tt, d_rnn), F32),
            pltpu.VMEM((tt, d_rnn), F32),
            pltpu.VMEM((tt, d_rnn), F32),
            pltpu.VMEM((SUBLANES, d_rnn), F32),
        ],
        compiler_params=_params("parallel", "arbitrary"),
        name="lru",
    )(x, conv0, h0.reshape(b, 1, d_rnn), row(g), w_in, cw, row(cb), gaw, row(gab), gxw, row(gxb),
      row(lam), w_out)
    return out, buf, hlast.reshape(b, d_rnn)


def _qkv_kernel(x_ref, g_ref, w_ref, q_ref, k_ref, v_ref, *rest, d, scale):
    xn = _rms(x_ref[...], g_ref[...]).astype(BF16)
    q = _dot(xn, w_ref[:, 0:d]) * scale
    k = _dot(xn, w_ref[:, d:2 * d])
    v = _dot(xn, w_ref[:, 2 * d:3 * d])
    q_ref[...] = q.astype(q_ref.dtype)
    k_ref[...] = k
    v_ref[...] = v
    if rest:
        kb_ref, vb_ref = rest
        kb_ref[...] = k.astype(BF16)
        vb_ref[...] = v.astype(BF16)


def _qkv(x, g, w, tm, q_dtype, with_bf16_kv):
    n, d = x.shape
    blk = pl.BlockSpec((tm, d), lambda i: (i, 0))
    out_shape = [jax.ShapeDtypeStruct((n, d), q_dtype), jax.ShapeDtypeStruct((n, d), F32),
                 jax.ShapeDtypeStruct((n, d), F32)]
    if with_bf16_kv:
        out_shape += [jax.ShapeDtypeStruct((n, d), BF16)] * 2
    return pl.pallas_call(
        functools.partial(_qkv_kernel, d=d, scale=HEAD_DIM ** -0.5),
        out_shape=tuple(out_shape),
        grid=(n // tm,),
        in_specs=[blk, _const_spec((1, d)), _const_spec(w.shape)],
        out_specs=tuple([blk] * len(out_shape)),
        compiler_params=_params("parallel"),
        name="qkv",
    )(x, g.reshape(1, d), w)


def _oproj_kernel(x_ref, o_ref, w_ref, y_ref):
    y_ref[...] = x_ref[...] + _dot(o_ref[...].astype(BF16), w_ref[...])


def _oproj(x, o, w, tm):
    n, d = x.shape
    blk = pl.BlockSpec((tm, d), lambda i: (i, 0))
    return pl.pallas_call(
        _oproj_kernel,
        out_shape=jax.ShapeDtypeStruct((n, d), F32),
        grid=(n // tm,),
        in_specs=[blk, pl.BlockSpec((tm, o.shape[1]), lambda i: (i, 0)), _const_spec(w.shape)],
        out_specs=blk,
        compiler_params=_params("parallel"),
        name="oproj",
    )(x, o, w)


def _diff_lambda(lq1_ref, lk1_ref, lq2_ref, lk2_ref, lam_init):
    e1 = jnp.exp(jnp.sum(lq1_ref[...] * lk1_ref[...], axis=-1, keepdims=True))
    e2 = jnp.exp(jnp.sum(lq2_ref[...] * lk2_ref[...], axis=-1, keepdims=True))
    return e1 - e2 + lam_init


def _subln(o, sg, lam_init):
    return _rms(o, sg) * (1.0 - lam_init)


def _pattn_kernel(slopes_ref, q_ref, k_ref, v_ref, lq1_ref, lk1_ref, lq2_ref, lk2_ref, sg_ref,
                  o_ref, m_ref, l_ref, acc_ref, *, tq, lam_init):
    h = pl.program_id(1)
    qi = pl.program_id(2)
    slope = slopes_ref[h]
    q = q_ref[...]
    lane = lax.broadcasted_iota(jnp.int32, (1, 2 * HEAD_DIM), 1)
    zero = jnp.zeros_like(q)
    qs = (jnp.where(lane < HEAD_DIM, q, zero), jnp.where(lane >= HEAD_DIM, q, zero))
    kiota = lax.broadcasted_iota(jnp.int32, (1, tq), 1)
    riota = lax.broadcasted_iota(jnp.int32, (tq, 1), 0)

    m_ref[...] = jnp.full(m_ref.shape, NEG_INF, F32)
    l_ref[...] = jnp.zeros(l_ref.shape, F32)
    acc_ref[...] = jnp.zeros(acc_ref.shape, F32)

    def block(kb, masked):
        off = pl.multiple_of(kb * tq, tq)
        k = k_ref[pl.ds(off, tq), :]
        v = v_ref[pl.ds(off, tq), :]
        bias = slope * ((kb - qi) * tq + kiota).astype(F32)
        for c in range(2):
            s = lax.dot_general(qs[c], k, _NT, preferred_element_type=F32) + bias
            if masked:
                s = jnp.where(kiota <= riota, s, NEG_INF)
            m_old = m_ref[c]
            m_new = jnp.maximum(m_old, jnp.max(s, axis=1, keepdims=True))
            alpha = jnp.exp(m_old - m_new)
            p = jnp.exp(s - m_new)
            l_ref[c] = alpha * l_ref[c] + jnp.sum(p, axis=1, keepdims=True)
            acc_ref[c] = alpha * acc_ref[c] + _dot(p.astype(BF16), v)
            m_ref[c] = m_new

    def body(kb, carry):
        block(kb, False)
        return carry

    lax.fori_loop(0, qi, body, 0)
    block(qi, True)

    lam = _diff_lambda(lq1_ref, lk1_ref, lq2_ref, lk2_ref, lam_init)
    o = acc_ref[0] / l_ref[0] - lam * (acc_ref[1] / l_ref[1])
    o_ref[...] = _subln(o, sg_ref[...], lam_init).astype(o_ref.dtype)


def _pattn(q, k, v, slopes, lq1, lk1, lq2, lk2, sg, lam_init, tq):
    b, t, d = q.shape
    nh = d // V_DIM
    row = lambda a: a.reshape(1, -1)
    qblk = pl.BlockSpec((None, tq, V_DIM), lambda bi, hi, qi: (bi, qi, hi))
    kvblk = pl.BlockSpec((None, t, V_DIM), lambda bi, hi, qi: (bi, 0, hi))
    small = _const_spec((1, HEAD_DIM))
    return pl.pallas_call(
        functools.partial(_pattn_kernel, tq=tq, lam_init=lam_init),
        out_shape=jax.ShapeDtypeStruct((b, t, d), BF16),
        grid=(b, nh, t // tq),
        in_specs=[pl.BlockSpec(memory_space=pltpu.SMEM), qblk, kvblk, kvblk,
                  small, small, small, small, _const_spec((1, V_DIM))],
        out_specs=qblk,
        scratch_shapes=[
            pltpu.VMEM((2, tq, 1), F32),
            pltpu.VMEM((2, tq, 1), F32),
            pltpu.VMEM((2, tq, V_DIM), F32),
        ],
        compiler_params=_params("parallel", "parallel", "arbitrary"),
        name="prompt_attn",
    )(slopes, q, k, v, row(lq1), row(lk1), row(lq2), row(lk2), row(sg))


def _sattn_kernel(pt_ref, q_ref, kn_ref, vn_ref, ck_ref, cv_ref, lq1_ref, lk1_ref, lq2_ref,
                  lk2_ref, sg_ref, o_ref, qm_ref, b0_ref, m_ref, l_ref, acc_ref,
                  *, n_heads, t_q, page, past, lam_init):
    del pt_ref
    p = pl.program_id(1)
    group = 2 * t_q
    rows = n_heads * group
    cols = page * n_heads
    rid = lax.broadcasted_iota(jnp.int32, (rows, 1), 0)
    hrow = rid // group
    slope = jnp.exp2(-(hrow + 1).astype(F32))

    @pl.when(p == 0)
    def _():
        q = q_ref[...].astype(BF16)
        lane = lax.broadcasted_iota(jnp.int32, (1, 2 * HEAD_DIM), 1)
        for hh in range(n_heads):
            qh = q[:, hh * V_DIM:(hh + 1) * V_DIM]
            zero = jnp.zeros_like(qh)
            qm_ref[hh * group:hh * group + t_q, :] = jnp.where(lane < HEAD_DIM, qh, zero)
            qm_ref[hh * group + t_q:(hh + 1) * group, :] = jnp.where(lane >= HEAD_DIM, qh, zero)
        cid = lax.broadcasted_iota(jnp.int32, (1, cols), 1)
        b0_ref[...] = jnp.where(cid % n_heads == hrow, slope * (cid // n_heads).astype(F32), NEG_INF)

        nid = lax.broadcasted_iota(jnp.int32, (1, t_q * n_heads), 1)
        valid = (nid % n_heads == hrow) & (nid // n_heads <= rid % t_q)
        s = lax.dot_general(qm_ref[...], kn_ref[...].astype(BF16), _NT, preferred_element_type=F32)
        s = jnp.where(valid, s + slope * (nid // n_heads).astype(F32), NEG_INF)
        m0 = jnp.max(s, axis=1, keepdims=True)
        pe = jnp.exp(s - m0)
        m_ref[...] = m0
        l_ref[...] = jnp.sum(pe, axis=1, keepdims=True)
        acc_ref[...] = _dot(pe.astype(BF16), vn_ref[...].astype(BF16))

    k2 = ck_ref[...].reshape(cols, V_DIM).astype(BF16)
    v2 = cv_ref[...].reshape(cols, V_DIM).astype(BF16)
    s = lax.dot_general(qm_ref[...], k2, _NT, preferred_element_type=F32) + b0_ref[...]
    c_p = slope * (p * page - past).astype(F32)
    m_old = m_ref[...]
    m_new = jnp.maximum(m_old, jnp.max(s, axis=1, keepdims=True) + c_p)
    alpha = jnp.exp(m_old - m_new)
    pe = jnp.exp(s - (m_new - c_p))
    l_ref[...] = alpha * l_ref[...] + jnp.sum(pe, axis=1, keepdims=True)
    acc_ref[...] = alpha * acc_ref[...] + _dot(pe.astype(BF16), v2)
    m_ref[...] = m_new

    @pl.when(p == pl.num_programs(1) - 1)
    def _():
        lam = _diff_lambda(lq1_ref, lk1_ref, lq2_ref, lk2_ref, lam_init)
        o = acc_ref[...] / l_ref[...]
        for hh in range(n_heads):
            o1 = o[hh * group:hh * group + t_q, :]
            o2 = o[hh * group + t_q:(hh + 1) * group, :]
            on = _subln(o1 - lam * o2, sg_ref[...], lam_init)
            o_ref[:, hh * V_DIM:(hh + 1) * V_DIM] = on.astype(o_ref.dtype)


def _sattn(q, k_new, v_new, cache_k, cache_v, layer, page_table, lq1, lk1, lq2, lk2, sg, lam_init):
    b, t_q, d = q.shape
    n_pages = page_table.shape[1]
    page, n_heads = cache_k.shape[2], cache_k.shape[3]
    rows = n_heads * 2 * t_q
    row = lambda a: a.reshape(1, -1)
    small = pl.BlockSpec((1, HEAD_DIM), lambda bi, pi, pt: (0, 0))
    newblk = pl.BlockSpec((None, t_q * n_heads, V_DIM), lambda bi, pi, pt: (bi, 0, 0))
    cblk = pl.BlockSpec((None, None, page, n_heads, V_DIM),
                        lambda bi, pi, pt: (layer, pt[bi, pi], 0, 0, 0))
    qblk = pl.BlockSpec((None, t_q, d), lambda bi, pi, pt: (bi, 0, 0))
    grid_spec = pltpu.PrefetchScalarGridSpec(
        num_scalar_prefetch=1,
        grid=(b, n_pages),
        in_specs=[qblk, newblk, newblk, cblk, cblk, small, small, small, small,
                  pl.BlockSpec((1, V_DIM), lambda bi, pi, pt: (0, 0))],
        out_specs=qblk,
        scratch_shapes=[
            pltpu.VMEM((rows, V_DIM), BF16),
            pltpu.VMEM((rows, page * n_heads), F32),
            pltpu.VMEM((rows, 1), F32),
            pltpu.VMEM((rows, 1), F32),
            pltpu.VMEM((rows, V_DIM), F32),
        ],
    )
    return pl.pallas_call(
        functools.partial(_sattn_kernel, n_heads=n_heads, t_q=t_q, page=page,
                          past=n_pages * page, lam_init=lam_init),
        out_shape=jax.ShapeDtypeStruct((b, t_q, d), F32),
        grid_spec=grid_spec,
        compiler_params=_params("parallel", "arbitrary"),
        name="sample_attn",
    )(page_table, q, k_new.reshape(b, t_q * n_heads, V_DIM), v_new.reshape(b, t_q * n_heads, V_DIM),
      cache_k, cache_v, row(lq1), row(lk1), row(lq2), row(lk2), row(sg))


def _tile(n, pref):
    return pref if n % pref == 0 else n


def kernel(x_prompt, x_sample, cache_k, cache_v, page_table, state_conv, state_h, p_prompt, p_sample, norm_g, final_norm_g, ffn1_w_in, ffn1_w_out, ffn2_w_in, ffn2_w_out, ple_w_gate, ple_w_proj, lru_w_in, lru_conv_w, lru_conv_b, lru_gate_a_w, lru_gate_a_b, lru_gate_x_w, lru_gate_x_b, lru_lambda, lru_w_out, attn_w_qkv, attn_lambda_q1, attn_lambda_k1, attn_lambda_q2, attn_lambda_k2, attn_subln_g, attn_w_o):
    bp, tp, d = x_prompt.shape
    bs, ts, _ = x_sample.shape
    depth = norm_g.shape[0]
    n_heads = cache_k.shape[3]
    d_rnn = lru_w_out.shape[1]
    np_, ns = bp * tp, bs * ts
    tm_p, tm_s = _tile(np_, 512), _tile(ns, 256)
    bf = lambda w: w.astype(BF16)

    xp = x_prompt.reshape(np_, d)
    xs = x_sample.reshape(ns, d)
    slopes = jnp.asarray(2.0 ** (-8.0 * (jnp.arange(n_heads) + 1) / n_heads), F32)
    conv_zero = jnp.zeros((bp, CONV_W - 1, d_rnn), F32)
    h_zero = jnp.zeros((bp, d_rnn), F32)

    kp_l, vp_l, ks_l, vs_l = [], [], [], []
    cp_l, hp_l, cs_l, hs_l = [], [], [], []
    for i in range(depth):
        j = i // 2
        w1i, w1o = bf(ffn1_w_in[i]), bf(ffn1_w_out[i])
        xp = _ffn(xp, norm_g[i, 0], w1i, w1o, tm_p)
        xs = _ffn(xs, norm_g[i, 0], w1i, w1o, tm_s)
        if i % 2 == 0:
            lw = (norm_g[i, 1], bf(lru_w_in[j]), lru_conv_w[j], lru_conv_b[j], bf(lru_gate_a_w[j]),
                  lru_gate_a_b[j], bf(lru_gate_x_w[j]), lru_gate_x_b[j], lru_lambda[j],
                  bf(lru_w_out[j]))
            op, buf_p, h_p = _lru(xp.reshape(bp, tp, d), conv_zero, h_zero, *lw, tt=_tile(tp, 256))
            os_, buf_s, h_s = _lru(xs.reshape(bs, ts, d), state_conv[j], state_h[j], *lw, tt=ts)
            xp, xs = op.reshape(np_, d), os_.reshape(ns, d)
            cp_l.append(buf_p); hp_l.append(h_p); cs_l.append(buf_s); hs_l.append(h_s)
        else:
            lam_init = 0.8 - 0.6 * math.exp(-0.3 * i)
            wqkv, wo = bf(attn_w_qkv[j]), bf(attn_w_o[j])
            lams = (attn_lambda_q1[j], attn_lambda_k1[j], attn_lambda_q2[j], attn_lambda_k2[j],
                    attn_subln_g[j])
            q_p, k_p, v_p, kb_p, vb_p = _qkv(xp, norm_g[i, 1], wqkv, tm_p, BF16, True)
            q_s, k_s, v_s = _qkv(xs, norm_g[i, 1], wqkv, tm_s, F32, False)
            o_p = _pattn(q_p.reshape(bp, tp, d), kb_p.reshape(bp, tp, d), vb_p.reshape(bp, tp, d),
                         slopes, *lams, lam_init=lam_init, tq=_tile(tp, 512))
            o_s = _sattn(q_s.reshape(bs, ts, d), k_s, v_s, cache_k, cache_v, j, page_table,
                         *lams, lam_init=lam_init)
            xp = _oproj(xp, o_p.reshape(np_, d), wo, tm_p)
            xs = _oproj(xs, o_s.reshape(ns, d), wo, tm_s)
            kp_l.append(k_p.reshape(bp, tp, n_heads, 2 * HEAD_DIM))
            vp_l.append(v_p.reshape(bp, tp, n_heads, V_DIM))
            ks_l.append(k_s.reshape(bs, ts, n_heads, 2 * HEAD_DIM))
            vs_l.append(v_s.reshape(bs, ts, n_heads, V_DIM))
        w2i, w2o = bf(ffn2_w_in[i]), bf(ffn2_w_out[i])
        xp = _ffn(xp, norm_g[i, 2], w2i, w2o, tm_p)
        xs = _ffn(xs, norm_g[i, 2], w2i, w2o, tm_s)
        last = i == depth - 1
        wg, wp = bf(ple_w_gate[i]), bf(ple_w_proj[i])
        xp = _ple(xp, p_prompt[i].reshape(np_, -1), norm_g[i, 3], wg, wp, final_norm_g, last, tm_p)
        xs = _ple(xs, p_sample[i].reshape(ns, -1), norm_g[i, 3], wg, wp, final_norm_g, last, tm_s)

    return (xp.reshape(bp, tp, d), xs.reshape(bs, ts, d),
            jnp.stack(kp_l), jnp.stack(vp_l), jnp.stack(ks_l), jnp.stack(vs_l),
            jnp.stack(cp_l), jnp.stack(hp_l), jnp.stack(cs_l), jnp.stack(hs_l))
```

```python
import functools
import math

import jax
import jax.numpy as jnp
from jax import lax
from jax.experimental import pallas as pl
from jax.experimental.pallas import tpu as pltpu

F32 = jnp.float32
BF16 = jnp.bfloat16

NORM_EPS = 1e-6
NEG_INF = -1e30
LRU_C = 8.0
CONV_W = 4
LRU_BLOCKS = 4
HEAD_DIM = 64
V_DIM = 128
LOG2E = math.log2(math.e)
KEY_BLOCK = 256
Q_STRIPE = 256
PAGES_PER_STEP = 4
SUBLANES = 8
VMEM_LIMIT_BYTES = 56 * 1024 * 1024

_NT = (((1,), (1,)), ((), ()))


def _params(*sem):
    return pltpu.CompilerParams(dimension_semantics=sem, vmem_limit_bytes=VMEM_LIMIT_BYTES)


def _const_spec(shape):
    nd = len(shape)
    return pl.BlockSpec(shape, lambda *_: (0,) * nd, pipeline_mode=pl.Buffered(1))


def _rms(x, g):
    y = x * lax.rsqrt(jnp.mean(x * x, axis=-1, keepdims=True) + NORM_EPS)
    return y * g


def _dot(a, b):
    return jnp.dot(a, b, preferred_element_type=F32)


def _ffn_kernel(x_ref, g_ref, win_ref, wout_ref, o_ref, *, d_ff, chunk):
    x = x_ref[...]
    xn = _rms(x, g_ref[...]).astype(BF16)
    acc = jnp.zeros(x.shape, F32)
    for c in range(d_ff // chunk):
        lo = c * chunk
        gt = _dot(xn, win_ref[:, lo:lo + chunk])
        ut = _dot(xn, win_ref[:, d_ff + lo:d_ff + lo + chunk])
        act = (gt * jax.nn.sigmoid(gt) * ut).astype(BF16)
        acc = acc + _dot(act, wout_ref[lo:lo + chunk, :])
    o_ref[...] = x + 0.5 * acc


def _ffn(x, g, w_in, w_out, tm):
    n, d = x.shape
    d_ff = w_out.shape[0]
    return pl.pallas_call(
        functools.partial(_ffn_kernel, d_ff=d_ff, chunk=256),
        out_shape=jax.ShapeDtypeStruct((n, d), F32),
        grid=(n // tm,),
        in_specs=[
            pl.BlockSpec((tm, d), lambda i: (i, 0)),
            _const_spec((1, d)),
            _const_spec(w_in.shape),
            _const_spec(w_out.shape),
        ],
        out_specs=pl.BlockSpec((tm, d), lambda i: (i, 0)),
        compiler_params=_params("parallel"),
        name="ffn",
    )(x, g.reshape(1, d), w_in, w_out)


def _ple_kernel(x_ref, p_ref, g_ref, wg_ref, wp_ref, fg_ref, o_ref, *, final_norm):
    x = x_ref[...]
    xn = _rms(x, g_ref[...]).astype(BF16)
    gate = jax.nn.sigmoid(_dot(xn, wg_ref[...]))
    proj = _dot(p_ref[...].astype(BF16), wp_ref[...])
    y = x + gate * proj
    if final_norm:
        y = _rms(y, fg_ref[...])
    o_ref[...] = y


def _ple(x, p, g, w_gate, w_proj, final_g, final_norm, tm):
    n, d = x.shape
    dp = p.shape[1]
    return pl.pallas_call(
        functools.partial(_ple_kernel, final_norm=final_norm),
        out_shape=jax.ShapeDtypeStruct((n, d), F32),
        grid=(n // tm,),
        in_specs=[
            pl.BlockSpec((tm, d), lambda i: (i, 0)),
            pl.BlockSpec((tm, dp), lambda i: (i, 0)),
            _const_spec((1, d)),
            _const_spec(w_gate.shape),
            _const_spec(w_proj.shape),
            _const_spec((1, d)),
        ],
        out_specs=pl.BlockSpec((tm, d), lambda i: (i, 0)),
        compiler_params=_params("parallel"),
        name="ple",
    )(x, p, g.reshape(1, d), w_gate, w_proj, final_g.reshape(1, d))


def _softplus(z):
    return jnp.maximum(z, 0.0) + jnp.log1p(jnp.exp(-jnp.abs(z)))


def _lru_kernel(x_ref, conv0_ref, h0_ref, g_ref, win_ref, cw_ref, cb_ref, gaw_ref, gab_ref,
                gxw_ref, gxb_ref, lam_ref, wout_ref,
                o_ref, buf_ref, hlast_ref,
                ubuf_ref, a_ref, b_ref, hs_ref, hcar_ref, *, tt, d_rnn):
    t = pl.program_id(1)
    tail = CONV_W - 1
    base = SUBLANES

    @pl.when(t == 0)
    def _():
        ubuf_ref[base - tail:base, :] = conv0_ref[...]
        hcar_ref[0:1, :] = h0_ref[...]

    x = x_ref[...]
    xn = _rms(x, g_ref[...]).astype(BF16)
    yu = _dot(xn, win_ref[...])
    y = jax.nn.gelu(yu[:, :d_rnn])
    u = yu[:, d_rnn:]
    ubuf_ref[base:base + tt, :] = u

    conv = cb_ref[...] + ubuf_ref[base - 3:base - 3 + tt, :] * cw_ref[0:1, :]
    conv = conv + ubuf_ref[base - 2:base - 2 + tt, :] * cw_ref[1:2, :]
    conv = conv + ubuf_ref[base - 1:base - 1 + tt, :] * cw_ref[2:3, :]
    conv = conv + u * cw_ref[3:4, :]

    bw = d_rnn // LRU_BLOCKS
    sp = _softplus(-lam_ref[...])
    for n in range(LRU_BLOCKS):
        sl = slice(n * bw, (n + 1) * bw)
        cn = conv[:, sl]
        cnb = cn.astype(BF16)
        r = jax.nn.sigmoid(_dot(cnb, gaw_ref[n]) + gab_ref[:, sl])
        i = jax.nn.sigmoid(_dot(cnb, gxw_ref[n]) + gxb_ref[:, sl])
        log_a = -LRU_C * r * sp[:, sl]
        a = jnp.exp(log_a)
        a_ref[:, sl] = a
        b_ref[:, sl] = jnp.sqrt(1.0 - a * a) * (i * cn)

    def group(gi, h):
        r0 = pl.multiple_of(gi * SUBLANES, SUBLANES)
        for j in range(SUBLANES):
            h = a_ref[pl.ds(r0 + j, 1), :] * h + b_ref[pl.ds(r0 + j, 1), :]
            hs_ref[pl.ds(r0 + j, 1), :] = h
        return h

    h = lax.fori_loop(0, tt // SUBLANES, group, hcar_ref[0:1, :])
    hcar_ref[0:1, :] = h

    z = (y * hs_ref[...]).astype(BF16)
    o_ref[...] = x + _dot(z, wout_ref[...])

    new_tail = ubuf_ref[base + tt - tail:base + tt, :]
    ubuf_ref[base - tail:base, :] = new_tail

    @pl.when(t == pl.num_programs(1) - 1)
    def _():
        buf_ref[...] = new_tail
        hlast_ref[...] = h


def _lru(x, conv0, h0, g, w_in, cw, cb, gaw, gab, gxw, gxb, lam, w_out, tt):
    b, t, d = x.shape
    d_rnn = w_out.shape[0]
    tail = CONV_W - 1
    row = lambda v: v.reshape(1, -1)
    out, buf, hlast = pl.pallas_call(
        functools.partial(_lru_kernel, tt=tt, d_rnn=d_rnn),
        out_shape=(
            jax.ShapeDtypeStruct((b, t, d), F32),
            jax.ShapeDtypeStruct((b, tail, d_rnn), F32),
            jax.ShapeDtypeStruct((b, 1, d_rnn), F32),
        ),
        grid=(b, t // tt),
        in_specs=[
            pl.BlockSpec((None, tt, d), lambda i, j: (i, j, 0)),
            pl.BlockSpec((None, tail, d_rnn), lambda i, j: (i, 0, 0)),
            pl.BlockSpec((None, 1, d_rnn), lambda i, j: (i, 0, 0)),
            _const_spec((1, d)),
            _const_spec(w_in.shape),
            _const_spec(cw.shape),
            _const_spec((1, d_rnn)),
            _const_spec(gaw.shape),
            _const_spec((1, d_rnn)),
            _const_spec(gxw.shape),
            _const_spec((1, d_rnn)),
            _const_spec((1, d_rnn)),
            _const_spec(w_out.shape),
        ],
        out_specs=(
            pl.BlockSpec((None, tt, d), lambda i, j: (i, j, 0)),
            pl.BlockSpec((None, tail, d_rnn), lambda i, j: (i, 0, 0)),
            pl.BlockSpec((None, 1, d_rnn), lambda i, j: (i, 0, 0)),
        ),
        scratch_shapes=[
            pltpu.VMEM((tt + 2 * SUBLANES, d_rnn), F32),
            pltpu.VMEM((tt, d_rnn), F32),
            pltpu.VMEM((tt, d_rnn), F32),
            pltpu.VMEM((tt, d_rnn), F32),
            pltpu.VMEM((SUBLANES, d_rnn), F32),
        ],
        compiler_params=_params("parallel", "arbitrary"),
        name="lru",
    )(x, conv0, h0.reshape(b, 1, d_rnn), row(g), w_in, cw, row(cb), gaw, row(gab), gxw, row(gxb),
      row(lam), w_out)
    return out, buf, hlast.reshape(b, d_rnn)


Q_SCALE = HEAD_DIM ** -0.5 * LOG2E


def _qkv_kernel(x_ref, g_ref, w_ref, q_ref, k_ref, v_ref, *rest, d):
    xn = _rms(x_ref[...], g_ref[...]).astype(BF16)
    q = _dot(xn, w_ref[:, 0:d]) * Q_SCALE
    k = _dot(xn, w_ref[:, d:2 * d])
    v = _dot(xn, w_ref[:, 2 * d:3 * d])
    q_ref[...] = q.astype(q_ref.dtype)
    k_ref[...] = k
    v_ref[...] = v
    if rest:
        kb_ref, vt_ref = rest
        kb_ref[...] = k.astype(BF16)
        n_heads, n_kb = vt_ref.shape[0], vt_ref.shape[1]
        for hh in range(n_heads):
            for jb in range(n_kb):
                blk = v[jb * KEY_BLOCK:(jb + 1) * KEY_BLOCK, hh * V_DIM:(hh + 1) * V_DIM]
                vt_ref[hh, jb] = blk.T.astype(BF16)


def _qkv_prompt(x, g, w, tm):
    b, t, d = x.shape
    nh = d // V_DIM
    blk = pl.BlockSpec((None, tm, d), lambda i, j: (i, j, 0))
    vtblk = pl.BlockSpec((None, nh, tm // KEY_BLOCK, V_DIM, KEY_BLOCK), lambda i, j: (i, 0, j, 0, 0))
    full = lambda dt: jax.ShapeDtypeStruct((b, t, d), dt)
    return pl.pallas_call(
        functools.partial(_qkv_kernel, d=d),
        out_shape=(full(BF16), full(F32), full(F32), full(BF16),
                   jax.ShapeDtypeStruct((b, nh, t // KEY_BLOCK, V_DIM, KEY_BLOCK), BF16)),
        grid=(b, t // tm),
        in_specs=[blk, _const_spec((1, d)), _const_spec(w.shape)],
        out_specs=(blk, blk, blk, blk, vtblk),
        compiler_params=_params("parallel", "parallel"),
        name="qkv_prompt",
    )(x, g.reshape(1, d), w)


def _qkv_sample(x, g, w, tm):
    n, d = x.shape
    blk = pl.BlockSpec((tm, d), lambda i: (i, 0))
    return pl.pallas_call(
        functools.partial(_qkv_kernel, d=d),
        out_shape=tuple([jax.ShapeDtypeStruct((n, d), F32)] * 3),
        grid=(n // tm,),
        in_specs=[blk, _const_spec((1, d)), _const_spec(w.shape)],
        out_specs=(blk, blk, blk),
        compiler_params=_params("parallel"),
        name="qkv_sample",
    )(x, g.reshape(1, d), w)


def _oproj_kernel(x_ref, o_ref, w_ref, y_ref):
    y_ref[...] = x_ref[...] + _dot(o_ref[...].astype(BF16), w_ref[...])


def _oproj(x, o, w, tm):
    n, d = x.shape
    blk = pl.BlockSpec((tm, d), lambda i: (i, 0))
    return pl.pallas_call(
        _oproj_kernel,
        out_shape=jax.ShapeDtypeStruct((n, d), F32),
        grid=(n // tm,),
        in_specs=[blk, pl.BlockSpec((tm, o.shape[1]), lambda i: (i, 0)), _const_spec(w.shape)],
        out_specs=blk,
        compiler_params=_params("parallel"),
        name="oproj",
    )(x, o, w)


def _diff_lambda(lq1_ref, lk1_ref, lq2_ref, lk2_ref, lam_init):
    e1 = jnp.exp(jnp.sum(lq1_ref[...] * lk1_ref[...], axis=-1, keepdims=True))
    e2 = jnp.exp(jnp.sum(lq2_ref[...] * lk2_ref[...], axis=-1, keepdims=True))
    return e1 - e2 + lam_init


def _subln(o, sg, lam_init):
    return _rms(o, sg) * (1.0 - lam_init)


def _pattn_kernel(slopes_ref, q_ref, k_ref, vt_ref, lq1_ref, lk1_ref, lq2_ref, lk2_ref, sg_ref,
                  o_ref, qz_ref, b0_ref, st_ref, m_ref, l_ref, acc_ref, *, tq, lam_init):
    h = pl.program_id(1)
    qi = pl.program_id(2)
    n_str = tq // Q_STRIPE
    slope = slopes_ref[h] * LOG2E

    q = q_ref[...]
    lane = lax.broadcasted_iota(jnp.int32, (1, 2 * HEAD_DIM), 1)
    zero = jnp.zeros_like(q)
    qz_ref[0] = jnp.where(lane < HEAD_DIM, q, zero)
    qz_ref[1] = jnp.where(lane >= HEAD_DIM, q, zero)
    kloc = lax.broadcasted_iota(jnp.int32, (KEY_BLOCK, Q_STRIPE), 0)
    qloc = lax.broadcasted_iota(jnp.int32, (KEY_BLOCK, Q_STRIPE), 1)
    b0_ref[...] = slope * kloc.astype(F32)
    m_ref[...] = jnp.full(m_ref.shape, NEG_INF, F32)
    l_ref[...] = jnp.zeros(l_ref.shape, F32)
    acc_ref[...] = jnp.zeros(acc_ref.shape, F32)

    def scores(c, s, kb):
        sl = slice(s * Q_STRIPE, (s + 1) * Q_STRIPE)
        koff = pl.multiple_of(kb * KEY_BLOCK, KEY_BLOCK)
        k = k_ref[pl.ds(koff, KEY_BLOCK), :]
        return lax.dot_general(k, qz_ref[c, sl, :], _NT, preferred_element_type=F32) + b0_ref[...]

    def update(st, c, s, kb, masked):
        sl = slice(s * Q_STRIPE, (s + 1) * Q_STRIPE)
        if masked:
            st = jnp.where(kloc <= qloc, st, NEG_INF)
        coff = slope * (kb * KEY_BLOCK - (qi * tq + s * Q_STRIPE)).astype(F32)
        m_old = m_ref[c, :, sl]
        m_new = jnp.maximum(m_old, jnp.max(st, axis=0, keepdims=True) + coff)
        alpha = jnp.exp2(m_old - m_new)
        p = jnp.exp2(st - (m_new - coff))
        l_ref[c, :, sl] = alpha * l_ref[c, :, sl] + jnp.sum(p, axis=0, keepdims=True)
        acc_ref[c, :, sl] = alpha * acc_ref[c, :, sl] + _dot(vt_ref[kb], p.astype(BF16))
        m_ref[c, :, sl] = m_new

    chains = [(c, s) for s in range(n_str) for c in range(2)]
    n_full = qi * n_str

    for i, (c, s) in enumerate(chains):
        st_ref[0, i] = scores(c, s, 0)

    def body(kk, carry):
        for slot in range(2):
            kb = 2 * kk + slot
            for i, (c, s) in enumerate(chains):
                st_ref[1 - slot, i] = scores(c, s, kb + 1)
                update(st_ref[slot, i], c, s, kb, False)
        return carry

    lax.fori_loop(0, n_full // 2, body, 0)

    for j in range(n_str):
        for i, (c, s) in enumerate(chains):
            if s >= j + 1 and j + 1 < n_str:
                st_ref[(j + 1) % 2, i] = scores(c, s, n_full + j + 1)
            if s >= j:
                update(st_ref[j % 2, i], c, s, n_full + j, s == j)

    lam = _diff_lambda(lq1_ref, lk1_ref, lq2_ref, lk2_ref, lam_init)
    o_t = acc_ref[0] * (1.0 / l_ref[0]) - lam * (acc_ref[1] * (1.0 / l_ref[1]))
    o_ref[...] = _subln(o_t.T, sg_ref[...], lam_init).astype(o_ref.dtype)


def _pattn(q, k, vt, slopes, lq1, lk1, lq2, lk2, sg, lam_init, tq):
    b, t, d = q.shape
    nh = d // V_DIM
    n_str = tq // Q_STRIPE
    assert tq % Q_STRIPE == 0 and n_str % 2 == 0 and KEY_BLOCK == Q_STRIPE
    row = lambda a: a.reshape(1, -1)
    qblk = pl.BlockSpec((None, tq, V_DIM), lambda bi, hi, qi: (bi, qi, hi))
    kblk = pl.BlockSpec((None, t, V_DIM), lambda bi, hi, qi: (bi, 0, hi))
    vblk = pl.BlockSpec((None, None, t // KEY_BLOCK, V_DIM, KEY_BLOCK), lambda bi, hi, qi: (bi, hi, 0, 0, 0))
    small = _const_spec((1, HEAD_DIM))
    return pl.pallas_call(
        functools.partial(_pattn_kernel, tq=tq, lam_init=lam_init),
        out_shape=jax.ShapeDtypeStruct((b, t, d), BF16),
        grid=(b, nh, t // tq),
        in_specs=[pl.BlockSpec(memory_space=pltpu.SMEM), qblk, kblk, vblk,
                  small, small, small, small, _const_spec((1, V_DIM))],
        out_specs=qblk,
        scratch_shapes=[
            pltpu.VMEM((2, tq, V_DIM), BF16),
            pltpu.VMEM((KEY_BLOCK, Q_STRIPE), F32),
            pltpu.VMEM((2, 2 * n_str, KEY_BLOCK, Q_STRIPE), F32),
            pltpu.VMEM((2, 1, tq), F32),
            pltpu.VMEM((2, 1, tq), F32),
            pltpu.VMEM((2, V_DIM, tq), F32),
        ],
        compiler_params=_params("parallel", "parallel", "arbitrary"),
        name="prompt_attn",
    )(slopes, q, k, vt, row(lq1), row(lk1), row(lq2), row(lk2), row(sg))


def _sattn_kernel(pt_ref, q_ref, kn_ref, vn_ref, *refs, n_heads, t_q, page, past, lam_init):
    del pt_ref
    npg = PAGES_PER_STEP
    ck_refs, cv_refs = refs[:npg], refs[npg:2 * npg]
    (lq1_ref, lk1_ref, lq2_ref, lk2_ref, sg_ref, o_ref,
     qm_ref, b0_ref, m_ref, l_ref, acc_ref) = refs[2 * npg:]
    p = pl.program_id(1)
    group = 2 * t_q
    rows = n_heads * group
    rid = lax.broadcasted_iota(jnp.int32, (rows, 1), 0)
    hrow = rid // group
    slope = jnp.exp2(-(hrow + 1).astype(F32)) * LOG2E

    @pl.when(p == 0)
    def _():
        q = q_ref[...].astype(BF16)
        lane = lax.broadcasted_iota(jnp.int32, (1, 2 * HEAD_DIM), 1)
        for hh in range(n_heads):
            qh = q[:, hh * V_DIM:(hh + 1) * V_DIM]
            zero = jnp.zeros_like(qh)
            qm_ref[hh * group:hh * group + t_q, :] = jnp.where(lane < HEAD_DIM, qh, zero)
            qm_ref[hh * group + t_q:(hh + 1) * group, :] = jnp.where(lane >= HEAD_DIM, qh, zero)
        cid = lax.broadcasted_iota(jnp.int32, (1, npg * page), 1)
        b0_ref[...] = slope * cid.astype(F32)

        nid = lax.broadcasted_iota(jnp.int32, (1, t_q * n_heads), 1)
        valid = (nid % n_heads == hrow) & (nid // n_heads <= rid % t_q)
        s = lax.dot_general(qm_ref[...], kn_ref[...].astype(BF16), _NT, preferred_element_type=F32)
        s = jnp.where(valid, s + slope * (nid // n_heads).astype(F32), NEG_INF)
        m0 = jnp.max(s, axis=1, keepdims=True)
        pe = jnp.exp2(s - m0)
        m_ref[...] = m0
        l_ref[...] = jnp.sum(pe, axis=1, keepdims=True)
        acc_ref[...] = _dot(pe.astype(BF16), vn_ref[...].astype(BF16))

    s_pages = []
    for j in range(npg):
        s_heads = []
        for hh in range(n_heads):
            kh = ck_refs[j][pl.ds(hh, page, stride=n_heads), :].astype(BF16)
            s_heads.append(lax.dot_general(qm_ref[hh * group:(hh + 1) * group, :], kh, _NT,
                                           preferred_element_type=F32))
        s_pages.append(jnp.concatenate(s_heads, axis=0))
    s = jnp.concatenate(s_pages, axis=1) + b0_ref[...]
    c_p = slope * (p * (npg * page) - past).astype(F32)
    m_old = m_ref[...]
    m_new = jnp.maximum(m_old, jnp.max(s, axis=1, keepdims=True) + c_p)
    alpha = jnp.exp2(m_old - m_new)
    pe = jnp.exp2(s - (m_new - c_p))
    l_ref[...] = alpha * l_ref[...] + jnp.sum(pe, axis=1, keepdims=True)
    peb = pe.astype(BF16)
    pv_heads = []
    for hh in range(n_heads):
        pv = None
        for j in range(npg):
            vh = cv_refs[j][pl.ds(hh, page, stride=n_heads), :].astype(BF16)
            d = _dot(peb[hh * group:(hh + 1) * group, j * page:(j + 1) * page], vh)
            pv = d if pv is None else pv + d
        pv_heads.append(pv)
    acc_ref[...] = alpha * acc_ref[...] + jnp.concatenate(pv_heads, axis=0)
    m_ref[...] = m_new

    @pl.when(p == pl.num_programs(1) - 1)
    def _():
        lam = _diff_lambda(lq1_ref, lk1_ref, lq2_ref, lk2_ref, lam_init)
        o = acc_ref[...] / l_ref[...]
        for hh in range(n_heads):
            o1 = o[hh * group:hh * group + t_q, :]
            o2 = o[hh * group + t_q:(hh + 1) * group, :]
            on = _subln(o1 - lam * o2, sg_ref[...], lam_init)
            o_ref[:, hh * V_DIM:(hh + 1) * V_DIM] = on.astype(o_ref.dtype)


def _sattn(q, k_new, v_new, cache_k, cache_v, layer, page_table, lq1, lk1, lq2, lk2, sg, lam_init):
    b, t_q, d = q.shape
    n_pages = page_table.shape[1]
    n_layers, n_phys, page, n_heads, _ = cache_k.shape
    npg = PAGES_PER_STEP
    assert n_pages % npg == 0
    rows = n_heads * 2 * t_q
    row = lambda a: a.reshape(1, -1)
    ck = cache_k.reshape(n_layers, n_phys, page * n_heads, V_DIM)
    cv = cache_v.reshape(n_layers, n_phys, page * n_heads, V_DIM)
    small = pl.BlockSpec((1, HEAD_DIM), lambda bi, pi, pt: (0, 0))
    newblk = pl.BlockSpec((None, t_q * n_heads, V_DIM), lambda bi, pi, pt: (bi, 0, 0))

    def cblk(j):
        return pl.BlockSpec((None, None, page * n_heads, V_DIM),
                            lambda bi, pi, pt: (layer, pt[bi, pi * npg + j], 0, 0))

    qblk = pl.BlockSpec((None, t_q, d), lambda bi, pi, pt: (bi, 0, 0))
    grid_spec = pltpu.PrefetchScalarGridSpec(
        num_scalar_prefetch=1,
        grid=(b, n_pages // npg),
        in_specs=[qblk, newblk, newblk] + [cblk(j) for j in range(npg)] * 2
        + [small, small, small, small, pl.BlockSpec((1, V_DIM), lambda bi, pi, pt: (0, 0))],
        out_specs=qblk,
        scratch_shapes=[
            pltpu.VMEM((rows, V_DIM), BF16),
            pltpu.VMEM((rows, npg * page), F32),
            pltpu.VMEM((rows, 1), F32),
            pltpu.VMEM((rows, 1), F32),
            pltpu.VMEM((rows, V_DIM), F32),
        ],
    )
    return pl.pallas_call(
        functools.partial(_sattn_kernel, n_heads=n_heads, t_q=t_q, page=page,
                          past=n_pages * page, lam_init=lam_init),
        out_shape=jax.ShapeDtypeStruct((b, t_q, d), F32),
        grid_spec=grid_spec,
        compiler_params=_params("parallel", "arbitrary"),
        name="sample_attn",
    )(page_table, q, k_new.reshape(b, t_q * n_heads, V_DIM), v_new.reshape(b, t_q * n_heads, V_DIM),
      *([ck] * npg), *([cv] * npg), row(lq1), row(lk1), row(lq2), row(lk2), row(sg))


def _tile(n, pref):
    return pref if n % pref == 0 else n


def kernel(x_prompt, x_sample, cache_k, cache_v, page_table, state_conv, state_h, p_prompt, p_sample, norm_g, final_norm_g, ffn1_w_in, ffn1_w_out, ffn2_w_in, ffn2_w_out, ple_w_gate, ple_w_proj, lru_w_in, lru_conv_w, lru_conv_b, lru_gate_a_w, lru_gate_a_b, lru_gate_x_w, lru_gate_x_b, lru_lambda, lru_w_out, attn_w_qkv, attn_lambda_q1, attn_lambda_k1, attn_lambda_q2, attn_lambda_k2, attn_subln_g, attn_w_o):
    bp, tp, d = x_prompt.shape
    bs, ts, _ = x_sample.shape
    depth = norm_g.shape[0]
    n_heads = cache_k.shape[3]
    d_rnn = lru_w_out.shape[1]
    np_, ns = bp * tp, bs * ts
    tm_p, tm_s = _tile(np_, 512), _tile(ns, 256)
    bf = lambda w: w.astype(BF16)

    xp = x_prompt.reshape(np_, d)
    xs = x_sample.reshape(ns, d)
    slopes = jnp.asarray(2.0 ** (-8.0 * (jnp.arange(n_heads) + 1) / n_heads), F32)
    conv_zero = jnp.zeros((bp, CONV_W - 1, d_rnn), F32)
    h_zero = jnp.zeros((bp, d_rnn), F32)

    kp_l, vp_l, ks_l, vs_l = [], [], [], []
    cp_l, hp_l, cs_l, hs_l = [], [], [], []
    for i in range(depth):
        j = i // 2
        w1i, w1o = bf(ffn1_w_in[i]), bf(ffn1_w_out[i])
        xp = _ffn(xp, norm_g[i, 0], w1i, w1o, tm_p)
        xs = _ffn(xs, norm_g[i, 0], w1i, w1o, tm_s)
        if i % 2 == 0:
            lw = (norm_g[i, 1], bf(lru_w_in[j]), lru_conv_w[j], lru_conv_b[j], bf(lru_gate_a_w[j]),
                  lru_gate_a_b[j], bf(lru_gate_x_w[j]), lru_gate_x_b[j], lru_lambda[j],
                  bf(lru_w_out[j]))
            op, buf_p, h_p = _lru(xp.reshape(bp, tp, d), conv_zero, h_zero, *lw, tt=_tile(tp, 256))
            os_, buf_s, h_s = _lru(xs.reshape(bs, ts, d), state_conv[j], state_h[j], *lw, tt=ts)
            xp, xs = op.reshape(np_, d), os_.reshape(ns, d)
            cp_l.append(buf_p); hp_l.append(h_p); cs_l.append(buf_s); hs_l.append(h_s)
        else:
            lam_init = 0.8 - 0.6 * math.exp(-0.3 * i)
            wqkv, wo = bf(attn_w_qkv[j]), bf(attn_w_o[j])
            lams = (attn_lambda_q1[j], attn_lambda_k1[j], attn_lambda_q2[j], attn_lambda_k2[j],
                    attn_subln_g[j])
            q_p, k_p, v_p, kb_p, vt_p = _qkv_prompt(xp.reshape(bp, tp, d), norm_g[i, 1], wqkv,
                                                    _tile(tp, 512))
            q_s, k_s, v_s = _qkv_sample(xs, norm_g[i, 1], wqkv, tm_s)
            o_p = _pattn(q_p, kb_p, vt_p, slopes, *lams, lam_init=lam_init, tq=_tile(tp, 1024))
            o_s = _sattn(q_s.reshape(bs, ts, d), k_s, v_s, cache_k, cache_v, j, page_table,
                         *lams, lam_init=lam_init)
            xp = _oproj(xp, o_p.reshape(np_, d), wo, tm_p)
            xs = _oproj(xs, o_s.reshape(ns, d), wo, tm_s)
            kp_l.append(k_p.reshape(bp, tp, n_heads, 2 * HEAD_DIM))
            vp_l.append(v_p.reshape(bp, tp, n_heads, V_DIM))
            ks_l.append(k_s.reshape(bs, ts, n_heads, 2 * HEAD_DIM))
            vs_l.append(v_s.reshape(bs, ts, n_heads, V_DIM))
        w2i, w2o = bf(ffn2_w_in[i]), bf(ffn2_w_out[i])
        xp = _ffn(xp, norm_g[i, 2], w2i, w2o, tm_p)
        xs = _ffn(xs, norm_g[i, 2], w2i, w2o, tm_s)
        last = i == depth - 1
        wg, wp = bf(ple_w_gate[i]), bf(ple_w_proj[i])
        xp = _ple(xp, p_prompt[i].reshape(np_, -1), norm_g[i, 3], wg, wp, final_norm_g, last, tm_p)
        xs = _ple(xs, p_sample[i].reshape(ns, -1), norm_g[i, 3], wg, wp, final_norm_g, last, tm_s)

    return (xp.reshape(bp, tp, d), xs.reshape(bs, ts, d),
            jnp.stack(kp_l), jnp.stack(vp_l), jnp.stack(ks_l), jnp.stack(vs_l),
            jnp.stack(cp_l), jnp.stack(hp_l), jnp.stack(cs_l), jnp.stack(hs_l))
```

```python
import functools
import math

import jax
import jax.numpy as jnp
from jax import lax
from jax.experimental import pallas as pl
from jax.experimental.pallas import tpu as pltpu

F32 = jnp.float32
BF16 = jnp.bfloat16

NORM_EPS = 1e-6
NEG_INF = -1e30
LRU_C = 8.0
CONV_W = 4
LRU_BLOCKS = 4
HEAD_DIM = 64
V_DIM = 128
LOG2E = math.log2(math.e)
KEY_BLOCK = 256
Q_STRIPE = 256
K_AUG = 2 * V_DIM
VT_ROWS = V_DIM + 16
POS_RADIX = 16
PAGES_PER_STEP = 8
FFN_CHUNK = 256
SUBLANES = 8
VMEM_LIMIT_BYTES = 56 * 1024 * 1024

_NT = (((1,), (1,)), ((), ()))
_TN = (((0,), (0,)), ((), ()))


def _params(*sem):
    return pltpu.CompilerParams(dimension_semantics=sem, vmem_limit_bytes=VMEM_LIMIT_BYTES)


def _const_spec(shape):
    nd = len(shape)
    return pl.BlockSpec(shape, lambda *_: (0,) * nd, pipeline_mode=pl.Buffered(1))


def _rms(x, g):
    y = x * lax.rsqrt(jnp.mean(x * x, axis=-1, keepdims=True) + NORM_EPS)
    return y * g


def _dot(a, b):
    return jnp.dot(a, b, preferred_element_type=F32)


def _swiglu_half(x, g, win_ref, wout_ref):
    d_ff = wout_ref.shape[0]
    xn = _rms(x, g).astype(BF16)
    acc = jnp.zeros(x.shape, F32)
    for c in range(d_ff // FFN_CHUNK):
        lo = c * FFN_CHUNK
        gt = _dot(xn, win_ref[:, lo:lo + FFN_CHUNK])
        ut = _dot(xn, win_ref[:, d_ff + lo:d_ff + lo + FFN_CHUNK])
        act = (gt * jax.nn.sigmoid(gt) * ut).astype(BF16)
        acc = acc + _dot(act, wout_ref[lo:lo + FFN_CHUNK, :])
    return x + 0.5 * acc


def _ffn_kernel(x_ref, g_ref, win_ref, wout_ref, o_ref):
    o_ref[...] = _swiglu_half(x_ref[...], g_ref[...], win_ref, wout_ref)


def _ffn(x, g, w_in, w_out, tm):
    n, d = x.shape
    return pl.pallas_call(
        _ffn_kernel,
        out_shape=jax.ShapeDtypeStruct((n, d), F32),
        grid=(n // tm,),
        in_specs=[
            pl.BlockSpec((tm, d), lambda i: (i, 0)),
            _const_spec((1, d)),
            _const_spec(w_in.shape),
            _const_spec(w_out.shape),
        ],
        out_specs=pl.BlockSpec((tm, d), lambda i: (i, 0)),
        compiler_params=_params("parallel"),
        name="ffn",
    )(x, g.reshape(1, d), w_in, w_out)


def _post_kernel(*refs, o_mode, final_norm):
    x_ref, refs = refs[0], refs[1:]
    if o_mode is not None:
        o_ref, wo_ref, refs = refs[0], refs[1], refs[2:]
    g2_ref, win_ref, wout_ref, p_ref, g3_ref, wg_ref, wp_ref, fg_ref, y_ref = refs
    x = x_ref[...]
    if o_mode == "transposed":
        nh, dv, tm = o_ref.shape
        x = x + lax.dot_general(o_ref[...].reshape(nh * dv, tm), wo_ref[...], _TN,
                                preferred_element_type=F32)
    elif o_mode == "rows":
        x = x + _dot(o_ref[...].astype(BF16), wo_ref[...])
    x = _swiglu_half(x, g2_ref[...], win_ref, wout_ref)
    xn = _rms(x, g3_ref[...]).astype(BF16)
    gate = jax.nn.sigmoid(_dot(xn, wg_ref[...]))
    y = x + gate * _dot(p_ref[...].astype(BF16), wp_ref[...])
    if final_norm:
        y = _rms(y, fg_ref[...])
    y_ref[...] = y


def _post(x, o, w_o, o_mode, g2, w_in, w_out, p_all, layer, g3, w_gate, w_proj, final_g, final_norm, tm):
    b, t, d = x.shape
    dp = p_all.shape[-1]
    blk = pl.BlockSpec((None, tm, d), lambda i, j: (i, j, 0))
    args, specs = [x], [blk]
    if o_mode == "transposed":
        nh = o.shape[1]
        args += [o, w_o]
        specs += [pl.BlockSpec((None, nh, V_DIM, tm), lambda i, j: (i, 0, 0, j)), _const_spec(w_o.shape)]
    elif o_mode == "rows":
        args += [o, w_o]
        specs += [blk, _const_spec(w_o.shape)]
    args += [g2.reshape(1, d), w_in, w_out, p_all, g3.reshape(1, d), w_gate, w_proj,
             final_g.reshape(1, d)]
    specs += [_const_spec((1, d)), _const_spec(w_in.shape), _const_spec(w_out.shape),
              pl.BlockSpec((None, None, tm, dp), lambda i, j: (layer, i, j, 0)),
              _const_spec((1, d)), _const_spec(w_gate.shape), _const_spec(w_proj.shape),
              _const_spec((1, d))]
    return pl.pallas_call(
        functools.partial(_post_kernel, o_mode=o_mode, final_norm=final_norm),
        out_shape=jax.ShapeDtypeStruct((b, t, d), F32),
        grid=(b, t // tm),
        in_specs=specs,
        out_specs=blk,
        compiler_params=_params("parallel", "parallel"),
        name="post",
    )(*args)


def _softplus(z):
    return jnp.maximum(z, 0.0) + jnp.log1p(jnp.exp(-jnp.abs(z)))


def _lru_kernel(x_ref, conv0_ref, h0_ref, g_ref, win_ref, cw_ref, cb_ref, gaw_ref, gab_ref,
                gxw_ref, gxb_ref, lam_ref, wout_ref,
                o_ref, buf_ref, hlast_ref,
                ubuf_ref, a_ref, b_ref, hs_ref, hcar_ref, *, tt, d_rnn):
    t = pl.program_id(1)
    tail = CONV_W - 1
    base = SUBLANES

    @pl.when(t == 0)
    def _():
        ubuf_ref[base - tail:base, :] = conv0_ref[...]
        hcar_ref[0:1, :] = h0_ref[...]

    x = x_ref[...]
    xn = _rms(x, g_ref[...]).astype(BF16)
    yu = _dot(xn, win_ref[...])
    y = jax.nn.gelu(yu[:, :d_rnn])
    u = yu[:, d_rnn:]
    ubuf_ref[base:base + tt, :] = u

    conv = cb_ref[...] + ubuf_ref[base - 3:base - 3 + tt, :] * cw_ref[0:1, :]
    conv = conv + ubuf_ref[base - 2:base - 2 + tt, :] * cw_ref[1:2, :]
    conv = conv + ubuf_ref[base - 1:base - 1 + tt, :] * cw_ref[2:3, :]
    conv = conv + u * cw_ref[3:4, :]

    bw = d_rnn // LRU_BLOCKS
    sp = _softplus(-lam_ref[...])
    for n in range(LRU_BLOCKS):
        sl = slice(n * bw, (n + 1) * bw)
        cn = conv[:, sl]
        cnb = cn.astype(BF16)
        r = jax.nn.sigmoid(_dot(cnb, gaw_ref[n]) + gab_ref[:, sl])
        i = jax.nn.sigmoid(_dot(cnb, gxw_ref[n]) + gxb_ref[:, sl])
        log_a = -LRU_C * r * sp[:, sl]
        a = jnp.exp(log_a)
        a_ref[:, sl] = a
        om = 1.0 - a * a
        mult = jnp.where(om > 0.0, om * lax.rsqrt(om), 0.0)
        b_ref[:, sl] = mult * (i * cn)

    h = hcar_ref[0:1, :]
    for r in range(tt):
        h = a_ref[r:r + 1, :] * h + b_ref[r:r + 1, :]
        hs_ref[r:r + 1, :] = h
    hcar_ref[0:1, :] = h

    z = (y * hs_ref[...]).astype(BF16)
    o_ref[...] = x + _dot(z, wout_ref[...])

    new_tail = ubuf_ref[base + tt - tail:base + tt, :]
    ubuf_ref[base - tail:base, :] = new_tail

    @pl.when(t == pl.num_programs(1) - 1)
    def _():
        buf_ref[...] = new_tail
        hlast_ref[...] = h


def _lru(x, conv0, h0, g, w_in, cw, cb, gaw, gab, gxw, gxb, lam, w_out, tt):
    b, t, d = x.shape
    d_rnn = w_out.shape[0]
    tail = CONV_W - 1
    row = lambda v: v.reshape(1, -1)
    out, buf, hlast = pl.pallas_call(
        functools.partial(_lru_kernel, tt=tt, d_rnn=d_rnn),
        out_shape=(
            jax.ShapeDtypeStruct((b, t, d), F32),
            jax.ShapeDtypeStruct((b, tail, d_rnn), F32),
            jax.ShapeDtypeStruct((b, 1, d_rnn), F32),
        ),
        grid=(b, t // tt),
        in_specs=[
            pl.BlockSpec((None, tt, d), lambda i, j: (i, j, 0)),
            pl.BlockSpec((None, tail, d_rnn), lambda i, j: (i, 0, 0)),
            pl.BlockSpec((None, 1, d_rnn), lambda i, j: (i, 0, 0)),
            _const_spec((1, d)),
            _const_spec(w_in.shape),
            _const_spec(cw.shape),
            _const_spec((1, d_rnn)),
            _const_spec(gaw.shape),
            _const_spec((1, d_rnn)),
            _const_spec(gxw.shape),
            _const_spec((1, d_rnn)),
            _const_spec((1, d_rnn)),
            _const_spec(w_out.shape),
        ],
        out_specs=(
            pl.BlockSpec((None, tt, d), lambda i, j: (i, j, 0)),
            pl.BlockSpec((None, tail, d_rnn), lambda i, j: (i, 0, 0)),
            pl.BlockSpec((None, 1, d_rnn), lambda i, j: (i, 0, 0)),
        ),
        scratch_shapes=[
            pltpu.VMEM((tt + 2 * SUBLANES, d_rnn), F32),
            pltpu.VMEM((tt, d_rnn), F32),
            pltpu.VMEM((tt, d_rnn), F32),
            pltpu.VMEM((tt, d_rnn), F32),
            pltpu.VMEM((SUBLANES, d_rnn), F32),
        ],
        compiler_params=_params("parallel", "arbitrary"),
        name="lru",
    )(x, conv0, h0.reshape(b, 1, d_rnn), row(g), w_in, cw, row(cb), gaw, row(gab), gxw, row(gxb),
      row(lam), w_out)
    return out, buf, hlast.reshape(b, d_rnn)


Q_SCALE = HEAD_DIM ** -0.5 * LOG2E


def _key_position_features(n_rows):
    r = lax.broadcasted_iota(jnp.int32, (n_rows, V_DIM), 0) % KEY_BLOCK
    lane = lax.broadcasted_iota(jnp.int32, (n_rows, V_DIM), 1)
    digit = jnp.where(lane < 2, r // POS_RADIX, r % POS_RADIX)
    return jnp.where(lane < 4, digit, 0).astype(F32).astype(BF16)


def _qkv_kernel(x_ref, g_ref, w_ref, q_ref, k_ref, v_ref, *rest, d):
    xn = _rms(x_ref[...], g_ref[...]).astype(BF16)
    q = _dot(xn, w_ref[:, 0:d]) * Q_SCALE
    k = _dot(xn, w_ref[:, d:2 * d])
    v = _dot(xn, w_ref[:, 2 * d:3 * d])
    q_ref[...] = q.astype(q_ref.dtype)
    k_ref[...] = k
    v_ref[...] = v
    if rest:
        ka_ref, vt_ref = rest
        n_heads, n_kb = vt_ref.shape[0], vt_ref.shape[1]
        feat = _key_position_features(k.shape[0])
        pad_rows = lax.broadcasted_iota(jnp.int32, (VT_ROWS - V_DIM, KEY_BLOCK), 0)
        ones_row = jnp.where(pad_rows == 0, 1.0, 0.0).astype(BF16)
        for hh in range(n_heads):
            ka_ref[:, hh * K_AUG:hh * K_AUG + V_DIM] = k[:, hh * V_DIM:(hh + 1) * V_DIM].astype(BF16)
            ka_ref[:, hh * K_AUG + V_DIM:(hh + 1) * K_AUG] = feat
            for jb in range(n_kb):
                blk = v[jb * KEY_BLOCK:(jb + 1) * KEY_BLOCK, hh * V_DIM:(hh + 1) * V_DIM]
                vt_ref[hh, jb, :V_DIM, :] = blk.T.astype(BF16)
                vt_ref[hh, jb, V_DIM:, :] = ones_row


def _qkv_prompt(x, g, w, tm):
    b, t, d = x.shape
    nh = d // V_DIM
    assert tm % KEY_BLOCK == 0
    blk = pl.BlockSpec((None, tm, d), lambda i, j: (i, j, 0))
    kablk = pl.BlockSpec((None, tm, nh * K_AUG), lambda i, j: (i, j, 0))
    vtblk = pl.BlockSpec((None, nh, tm // KEY_BLOCK, VT_ROWS, KEY_BLOCK), lambda i, j: (i, 0, j, 0, 0))
    full = lambda dt: jax.ShapeDtypeStruct((b, t, d), dt)
    return pl.pallas_call(
        functools.partial(_qkv_kernel, d=d),
        out_shape=(full(BF16), full(F32), full(F32),
                   jax.ShapeDtypeStruct((b, t, nh * K_AUG), BF16),
                   jax.ShapeDtypeStruct((b, nh, t // KEY_BLOCK, VT_ROWS, KEY_BLOCK), BF16)),
        grid=(b, t // tm),
        in_specs=[blk, _const_spec((1, d)), _const_spec(w.shape)],
        out_specs=(blk, blk, blk, kablk, vtblk),
        compiler_params=_params("parallel", "parallel"),
        name="qkv_prompt",
    )(x, g.reshape(1, d), w)


def _qkv_sample(x, g, w, tm):
    n, d = x.shape
    blk = pl.BlockSpec((tm, d), lambda i: (i, 0))
    return pl.pallas_call(
        functools.partial(_qkv_kernel, d=d),
        out_shape=tuple([jax.ShapeDtypeStruct((n, d), F32)] * 3),
        grid=(n // tm,),
        in_specs=[blk, _const_spec((1, d)), _const_spec(w.shape)],
        out_specs=(blk, blk, blk),
        compiler_params=_params("parallel"),
        name="qkv_sample",
    )(x, g.reshape(1, d), w)


def _diff_lambda(lq1_ref, lk1_ref, lq2_ref, lk2_ref, lam_init):
    e1 = jnp.exp(jnp.sum(lq1_ref[...] * lk1_ref[...], axis=-1, keepdims=True))
    e2 = jnp.exp(jnp.sum(lq2_ref[...] * lk2_ref[...], axis=-1, keepdims=True))
    return e1 - e2 + lam_init


def _subln(o, sg, lam_init):
    return _rms(o, sg) * (1.0 - lam_init)


def _pattn_kernel(slopes_ref, q_ref, k_ref, vt_ref, lq1_ref, lk1_ref, lq2_ref, lk2_ref, sg_ref,
                  o_ref, qz_ref, st_ref, m_ref, acc_ref, *, tq, lam_init):
    h = pl.program_id(1)
    qi = pl.program_id(2)
    n_str = tq // Q_STRIPE
    slope = slopes_ref[h] * LOG2E

    q = q_ref[...]
    lane = lax.broadcasted_iota(jnp.int32, (1, 2 * HEAD_DIM), 1)
    zero = jnp.zeros_like(q)
    s_vec = jnp.full((1, V_DIM), slope, F32)
    s_hi = s_vec.astype(BF16).astype(F32)
    s_lo = (s_vec - s_hi).astype(BF16).astype(F32)
    w = jnp.where(lane == 0, s_hi * POS_RADIX, jnp.where(lane == 1, s_lo * POS_RADIX,
                  jnp.where(lane == 2, s_hi, jnp.where(lane == 3, s_lo, 0.0)))).astype(BF16)
    wq = jnp.broadcast_to(w, q.shape)
    qz_ref[0, :, :V_DIM] = jnp.where(lane < HEAD_DIM, q, zero)
    qz_ref[1, :, :V_DIM] = jnp.where(lane >= HEAD_DIM, q, zero)
    qz_ref[0, :, V_DIM:] = wq
    qz_ref[1, :, V_DIM:] = wq
    kloc = lax.broadcasted_iota(jnp.int32, (KEY_BLOCK, Q_STRIPE), 0)
    qloc = lax.broadcasted_iota(jnp.int32, (KEY_BLOCK, Q_STRIPE), 1)
    m_ref[...] = jnp.full(m_ref.shape, NEG_INF, F32)
    acc_ref[...] = jnp.zeros(acc_ref.shape, F32)

    def scores(c, s, kb):
        sl = slice(s * Q_STRIPE, (s + 1) * Q_STRIPE)
        koff = pl.multiple_of(kb * KEY_BLOCK, KEY_BLOCK)
        k = k_ref[pl.ds(koff, KEY_BLOCK), :]
        return lax.dot_general(k, qz_ref[c, sl, :], _NT, preferred_element_type=F32)

    def update(st, c, s, kb, masked):
        sl = slice(s * Q_STRIPE, (s + 1) * Q_STRIPE)
        if masked:
            st = jnp.where(kloc <= qloc, st, NEG_INF)
        coff = slope * (kb * KEY_BLOCK - (qi * tq + s * Q_STRIPE)).astype(F32)
        m_old = m_ref[c, :, sl]
        m_new = jnp.maximum(m_old, jnp.max(st, axis=0, keepdims=True) + coff)
        alpha = jnp.exp2(m_old - m_new)
        p = jnp.exp2(st - (m_new - coff))
        acc_ref[c, :, sl] = alpha * acc_ref[c, :, sl] + _dot(vt_ref[kb], p.astype(BF16))
        m_ref[c, :, sl] = m_new

    chains = [(c, s) for s in range(n_str) for c in range(2)]
    n_full = qi * n_str

    for i, (c, s) in enumerate(chains):
        st_ref[0, i] = scores(c, s, 0)

    def body(kk, carry):
        for slot in range(2):
            kb = 2 * kk + slot
            for i, (c, s) in enumerate(chains):
                st_ref[1 - slot, i] = scores(c, s, kb + 1)
                update(st_ref[slot, i], c, s, kb, False)
        return carry

    lax.fori_loop(0, n_full // 2, body, 0)

    for j in range(n_str):
        for i, (c, s) in enumerate(chains):
            if s >= j + 1 and j + 1 < n_str:
                st_ref[(j + 1) % 2, i] = scores(c, s, n_full + j + 1)
            if s >= j:
                update(st_ref[j % 2, i], c, s, n_full + j, s == j)

    lam = _diff_lambda(lq1_ref, lk1_ref, lq2_ref, lk2_ref, lam_init)
    inv1 = 1.0 / acc_ref[0, V_DIM:V_DIM + 1, :]
    inv2 = lam * (1.0 / acc_ref[1, V_DIM:V_DIM + 1, :])
    o = acc_ref[0, :V_DIM, :] * inv1 - acc_ref[1, :V_DIM, :] * inv2
    rs = lax.rsqrt(jnp.mean(o * o, axis=0, keepdims=True) + NORM_EPS)
    o_ref[...] = (o * rs * (sg_ref[...] * (1.0 - lam_init))).astype(o_ref.dtype)


def _pattn(q, k, vt, slopes, lq1, lk1, lq2, lk2, sg, lam_init, tq):
    b, t, d = q.shape
    nh = d // V_DIM
    n_str = tq // Q_STRIPE
    assert tq % Q_STRIPE == 0 and n_str % 2 == 0 and KEY_BLOCK == Q_STRIPE
    row = lambda a: a.reshape(1, -1)
    qblk = pl.BlockSpec((None, tq, V_DIM), lambda bi, hi, qi: (bi, qi, hi))
    kblk = pl.BlockSpec((None, t, K_AUG), lambda bi, hi, qi: (bi, 0, hi))
    vblk = pl.BlockSpec((None, None, t // KEY_BLOCK, VT_ROWS, KEY_BLOCK),
                        lambda bi, hi, qi: (bi, hi, 0, 0, 0))
    small = _const_spec((1, HEAD_DIM))
    return pl.pallas_call(
        functools.partial(_pattn_kernel, tq=tq, lam_init=lam_init),
        out_shape=jax.ShapeDtypeStruct((b, nh, V_DIM, t), BF16),
        grid=(b, nh, t // tq),
        in_specs=[pl.BlockSpec(memory_space=pltpu.SMEM), qblk, kblk, vblk,
                  small, small, small, small, _const_spec((V_DIM, 1))],
        out_specs=pl.BlockSpec((None, None, V_DIM, tq), lambda bi, hi, qi: (bi, hi, 0, qi)),
        scratch_shapes=[
            pltpu.VMEM((2, tq, K_AUG), BF16),
            pltpu.VMEM((2, 2 * n_str, KEY_BLOCK, Q_STRIPE), F32),
            pltpu.VMEM((2, 1, tq), F32),
            pltpu.VMEM((2, VT_ROWS, tq), F32),
        ],
        compiler_params=_params("parallel", "parallel", "arbitrary"),
        name="prompt_attn",
    )(slopes, q, k, vt, row(lq1), row(lk1), row(lq2), row(lk2), sg.reshape(-1, 1))


def _sattn_kernel(pt_ref, q_ref, kn_ref, vn_ref, *refs, n_heads, t_q, page, past, lam_init):
    del pt_ref
    npg = PAGES_PER_STEP
    ck_refs, cv_refs = refs[:npg], refs[npg:2 * npg]
    (lq1_ref, lk1_ref, lq2_ref, lk2_ref, sg_ref, o_ref,
     qm_ref, b0_ref, m_ref, l_ref, acc_ref) = refs[2 * npg:]
    p = pl.program_id(1)
    group = 2 * t_q
    rows = n_heads * group
    rid = lax.broadcasted_iota(jnp.int32, (rows, 1), 0)
    hrow = rid // group
    slope = jnp.exp2(-(hrow + 1).astype(F32)) * LOG2E

    @pl.when(p == 0)
    def _():
        q = q_ref[...].astype(BF16)
        lane = lax.broadcasted_iota(jnp.int32, (1, 2 * HEAD_DIM), 1)
        for hh in range(n_heads):
            qh = q[:, hh * V_DIM:(hh + 1) * V_DIM]
            zero = jnp.zeros_like(qh)
            qm_ref[hh * group:hh * group + t_q, :] = jnp.where(lane < HEAD_DIM, qh, zero)
            qm_ref[hh * group + t_q:(hh + 1) * group, :] = jnp.where(lane >= HEAD_DIM, qh, zero)
        cid = lax.broadcasted_iota(jnp.int32, (1, npg * page), 1)
        b0_ref[...] = slope * cid.astype(F32)

        nid = lax.broadcasted_iota(jnp.int32, (1, t_q * n_heads), 1)
        valid = (nid % n_heads == hrow) & (nid // n_heads <= rid % t_q)
        s = lax.dot_general(qm_ref[...], kn_ref[...].astype(BF16), _NT, preferred_element_type=F32)
        s = jnp.where(valid, s + slope * (nid // n_heads).astype(F32), NEG_INF)
        m0 = jnp.max(s, axis=1, keepdims=True)
        pe = jnp.exp2(s - m0)
        m_ref[...] = m0
        l_ref[...] = jnp.sum(pe, axis=1, keepdims=True)
        acc_ref[...] = _dot(pe.astype(BF16), vn_ref[...].astype(BF16))

    s_pages = []
    for j in range(npg):
        s_heads = []
        for hh in range(n_heads):
            kt = ck_refs[j][pl.ds(hh, page, stride=n_heads), :].T.astype(BF16)
            s_heads.append(_dot(qm_ref[hh * group:(hh + 1) * group, :], kt))
        s_pages.append(jnp.concatenate(s_heads, axis=0))
    s = jnp.concatenate(s_pages, axis=1) + b0_ref[...]
    c_p = slope * (p * (npg * page) - past).astype(F32)
    m_old = m_ref[...]
    m_new = jnp.maximum(m_old, jnp.max(s, axis=1, keepdims=True) + c_p)
    alpha = jnp.exp2(m_old - m_new)
    pe = jnp.exp2(s - (m_new - c_p))
    l_ref[...] = alpha * l_ref[...] + jnp.sum(pe, axis=1, keepdims=True)
    peb = pe.astype(BF16)
    pv_heads = []
    for hh in range(n_heads):
        pv = None
        for j in range(npg):
            vh = cv_refs[j][pl.ds(hh, page, stride=n_heads), :].astype(BF16)
            d = _dot(peb[hh * group:(hh + 1) * group, j * page:(j + 1) * page], vh)
            pv = d if pv is None else pv + d
        pv_heads.append(pv)
    acc_ref[...] = alpha * acc_ref[...] + jnp.concatenate(pv_heads, axis=0)
    m_ref[...] = m_new

    @pl.when(p == pl.num_programs(1) - 1)
    def _():
        lam = _diff_lambda(lq1_ref, lk1_ref, lq2_ref, lk2_ref, lam_init)
        o = acc_ref[...] / l_ref[...]
        for hh in range(n_heads):
            o1 = o[hh * group:hh * group + t_q, :]
            o2 = o[hh * group + t_q:(hh + 1) * group, :]
            on = _subln(o1 - lam * o2, sg_ref[...], lam_init)
            o_ref[:, hh * V_DIM:(hh + 1) * V_DIM] = on.astype(o_ref.dtype)


def _sattn(q, k_new, v_new, cache_k, cache_v, layer, page_table, lq1, lk1, lq2, lk2, sg, lam_init):
    b, t_q, d = q.shape
    n_pages = page_table.shape[1]
    n_layers, n_phys, page, n_heads, _ = cache_k.shape
    npg = PAGES_PER_STEP
    assert n_pages % npg == 0
    rows = n_heads * 2 * t_q
    row = lambda a: a.reshape(1, -1)
    ck = cache_k.reshape(n_layers, n_phys, page * n_heads, V_DIM)
    cv = cache_v.reshape(n_layers, n_phys, page * n_heads, V_DIM)
    small = pl.BlockSpec((1, HEAD_DIM), lambda bi, pi, pt: (0, 0))
    newblk = pl.BlockSpec((None, t_q * n_heads, V_DIM), lambda bi, pi, pt: (bi, 0, 0))

    def cblk(j):
        return pl.BlockSpec((None, None, page * n_heads, V_DIM),
                            lambda bi, pi, pt: (layer, pt[bi, pi * npg + j], 0, 0))

    qblk = pl.BlockSpec((None, t_q, d), lambda bi, pi, pt: (bi, 0, 0))
    grid_spec = pltpu.PrefetchScalarGridSpec(
        num_scalar_prefetch=1,
        grid=(b, n_pages // npg),
        in_specs=[qblk, newblk, newblk] + [cblk(j) for j in range(npg)] * 2
        + [small, small, small, small, pl.BlockSpec((1, V_DIM), lambda bi, pi, pt: (0, 0))],
        out_specs=qblk,
        scratch_shapes=[
            pltpu.VMEM((rows, V_DIM), BF16),
            pltpu.VMEM((rows, npg * page), F32),
            pltpu.VMEM((rows, 1), F32),
            pltpu.VMEM((rows, 1), F32),
            pltpu.VMEM((rows, V_DIM), F32),
        ],
    )
    return pl.pallas_call(
        functools.partial(_sattn_kernel, n_heads=n_heads, t_q=t_q, page=page,
                          past=n_pages * page, lam_init=lam_init),
        out_shape=jax.ShapeDtypeStruct((b, t_q, d), F32),
        grid_spec=grid_spec,
        compiler_params=_params("parallel", "arbitrary"),
        name="sample_attn",
    )(page_table, q, k_new.reshape(b, t_q * n_heads, V_DIM), v_new.reshape(b, t_q * n_heads, V_DIM),
      *([ck] * npg), *([cv] * npg), row(lq1), row(lk1), row(lq2), row(lk2), row(sg))


def _tile(n, pref):
    return pref if n % pref == 0 else n


def kernel(x_prompt, x_sample, cache_k, cache_v, page_table, state_conv, state_h, p_prompt, p_sample, norm_g, final_norm_g, ffn1_w_in, ffn1_w_out, ffn2_w_in, ffn2_w_out, ple_w_gate, ple_w_proj, lru_w_in, lru_conv_w, lru_conv_b, lru_gate_a_w, lru_gate_a_b, lru_gate_x_w, lru_gate_x_b, lru_lambda, lru_w_out, attn_w_qkv, attn_lambda_q1, attn_lambda_k1, attn_lambda_q2, attn_lambda_k2, attn_subln_g, attn_w_o):
    bp, tp, d = x_prompt.shape
    bs, ts, _ = x_sample.shape
    depth = norm_g.shape[0]
    n_heads = cache_k.shape[3]
    d_rnn = lru_w_out.shape[1]
    np_, ns = bp * tp, bs * ts
    tm_p, tm_s = _tile(np_, 512), _tile(ns, 256)
    tt_p = _tile(tp, 512)
    bf = lambda w: w.astype(BF16)

    xp = x_prompt.reshape(np_, d)
    xs = x_sample.reshape(ns, d)
    ps_all = p_sample.reshape(depth, 1, ns, -1)
    slopes = jnp.asarray(2.0 ** (-8.0 * (jnp.arange(n_heads) + 1) / n_heads), F32)
    conv_zero = jnp.zeros((bp, CONV_W - 1, d_rnn), F32)
    h_zero = jnp.zeros((bp, d_rnn), F32)

    kp_l, vp_l, ks_l, vs_l = [], [], [], []
    cp_l, hp_l, cs_l, hs_l = [], [], [], []
    for i in range(depth):
        j = i // 2
        last = i == depth - 1
        w1i, w1o = bf(ffn1_w_in[i]), bf(ffn1_w_out[i])
        xp = _ffn(xp, norm_g[i, 0], w1i, w1o, tm_p)
        xs = _ffn(xs, norm_g[i, 0], w1i, w1o, tm_s)
        post_w = (norm_g[i, 2], bf(ffn2_w_in[i]), bf(ffn2_w_out[i]))
        ple_w = (norm_g[i, 3], bf(ple_w_gate[i]), bf(ple_w_proj[i]), final_norm_g, last)
        if i % 2 == 0:
            lw = (norm_g[i, 1], bf(lru_w_in[j]), lru_conv_w[j], lru_conv_b[j], bf(lru_gate_a_w[j]),
                  lru_gate_a_b[j], bf(lru_gate_x_w[j]), lru_gate_x_b[j], lru_lambda[j],
                  bf(lru_w_out[j]))
            xp3, buf_p, h_p = _lru(xp.reshape(bp, tp, d), conv_zero, h_zero, *lw, tt=_tile(tp, 256))
            xs3, buf_s, h_s = _lru(xs.reshape(bs, ts, d), state_conv[j], state_h[j], *lw, tt=ts)
            cp_l.append(buf_p); hp_l.append(h_p); cs_l.append(buf_s); hs_l.append(h_s)
            xp3 = _post(xp3, None, None, None, *post_w, p_prompt, i, *ple_w, tm=tt_p)
            xs3 = _post(xs3.reshape(1, ns, d), None, None, None, *post_w, ps_all, i, *ple_w, tm=tm_s)
        else:
            lam_init = 0.8 - 0.6 * math.exp(-0.3 * i)
            wqkv, wo = bf(attn_w_qkv[j]), bf(attn_w_o[j])
            lams = (attn_lambda_q1[j], attn_lambda_k1[j], attn_lambda_q2[j], attn_lambda_k2[j],
                    attn_subln_g[j])
            xp3 = xp.reshape(bp, tp, d)
            q_p, k_p, v_p, ka_p, vt_p = _qkv_prompt(xp3, norm_g[i, 1], wqkv, tt_p)
            q_s, k_s, v_s = _qkv_sample(xs, norm_g[i, 1], wqkv, tm_s)
            o_p = _pattn(q_p, ka_p, vt_p, slopes, *lams, lam_init=lam_init, tq=_tile(tp, 2048))
            o_s = _sattn(q_s.reshape(bs, ts, d), k_s, v_s, cache_k, cache_v, j, page_table,
                         *lams, lam_init=lam_init)
            xp3 = _post(xp3, o_p, wo, "transposed", *post_w, p_prompt, i, *ple_w, tm=tt_p)
            xs3 = _post(xs.reshape(1, ns, d), o_s.reshape(1, ns, d), wo, "rows", *post_w, ps_all, i,
                        *ple_w, tm=tm_s)
            kp_l.append(k_p.reshape(bp, tp, n_heads, 2 * HEAD_DIM))
            vp_l.append(v_p.reshape(bp, tp, n_heads, V_DIM))
            ks_l.append(k_s.reshape(bs, ts, n_heads, 2 * HEAD_DIM))
            vs_l.append(v_s.reshape(bs, ts, n_heads, V_DIM))
        xp, xs = xp3.reshape(np_, d), xs3.reshape(ns, d)

    return (xp.reshape(bp, tp, d), xs.reshape(bs, ts, d),
            jnp.stack(kp_l), jnp.stack(vp_l), jnp.stack(ks_l), jnp.stack(vs_l),
            jnp.stack(cp_l), jnp.stack(hp_l), jnp.stack(cs_l), jnp.stack(hs_l))
```

```python
import functools
import math

import jax
import jax.numpy as jnp
from jax import lax
from jax.experimental import pallas as pl
from jax.experimental.pallas import tpu as pltpu

F32 = jnp.float32
BF16 = jnp.bfloat16

NORM_EPS = 1e-6
NEG_INF = -1e30
LRU_C = 8.0
CONV_W = 4
LRU_BLOCKS = 4
HEAD_DIM = 64
V_DIM = 128
LOG2E = math.log2(math.e)
KEY_BLOCK = 256
Q_STRIPE = 256
K_AUG = 2 * V_DIM
VT_ROWS = V_DIM + 16
POS_RADIX = 16
PAGES_PER_STEP = 8
FFN_CHUNK = 256
VMEM_LIMIT_BYTES = 56 * 1024 * 1024

_NT = (((1,), (1,)), ((), ()))
_TN = (((0,), (0,)), ((), ()))


def _params(*sem):
    return pltpu.CompilerParams(dimension_semantics=sem, vmem_limit_bytes=VMEM_LIMIT_BYTES)


def _const_spec(shape):
    nd = len(shape)
    return pl.BlockSpec(shape, lambda *_: (0,) * nd, pipeline_mode=pl.Buffered(1))


def _layer_spec(w_all, layer):
    nd = w_all.ndim
    return pl.BlockSpec((None,) + w_all.shape[1:], lambda *_: (layer,) + (0,) * (nd - 1),
                        pipeline_mode=pl.Buffered(1))


def _wspec(w):
    return _layer_spec(*w)


def _rms(x, g):
    y = x * lax.rsqrt(jnp.mean(x * x, axis=-1, keepdims=True) + NORM_EPS)
    return y * g


def _dot(a, b):
    return jnp.dot(a, b, preferred_element_type=F32)


def _swiglu_half(x, g, win_ref, wout_ref):
    d_ff = wout_ref.shape[0]
    xn = _rms(x, g).astype(BF16)
    acc = jnp.zeros(x.shape, F32)
    for c in range(d_ff // FFN_CHUNK):
        lo = c * FFN_CHUNK
        gt = _dot(xn, win_ref[:, lo:lo + FFN_CHUNK])
        ut = _dot(xn, win_ref[:, d_ff + lo:d_ff + lo + FFN_CHUNK])
        act = (gt * jax.nn.sigmoid(gt) * ut).astype(BF16)
        acc = acc + _dot(act, wout_ref[lo:lo + FFN_CHUNK, :])
    return x + 0.5 * acc


def _ffn_kernel(x_ref, g_ref, win_ref, wout_ref, o_ref):
    o_ref[...] = _swiglu_half(x_ref[...], g_ref[...], win_ref, wout_ref)


def _ffn(x, g, w_in, w_out, tm):
    n, d = x.shape
    return pl.pallas_call(
        _ffn_kernel,
        out_shape=jax.ShapeDtypeStruct((n, d), F32),
        grid=(n // tm,),
        in_specs=[
            pl.BlockSpec((tm, d), lambda i: (i, 0)),
            _const_spec((1, d)),
            _wspec(w_in),
            _wspec(w_out),
        ],
        out_specs=pl.BlockSpec((tm, d), lambda i: (i, 0)),
        compiler_params=_params("parallel"),
        name="ffn",
    )(x, g.reshape(1, d), w_in[0], w_out[0])


def _post_kernel(*refs, o_mode, final_norm):
    x_ref, refs = refs[0], refs[1:]
    if o_mode is not None:
        o_ref, wo_ref, refs = refs[0], refs[1], refs[2:]
    g2_ref, win_ref, wout_ref, p_ref, g3_ref, wg_ref, wp_ref, fg_ref, y_ref = refs
    x = x_ref[...]
    if o_mode == "transposed":
        nh, dv, tm = o_ref.shape
        x = x + lax.dot_general(o_ref[...].reshape(nh * dv, tm), wo_ref[...], _TN,
                                preferred_element_type=F32)
    elif o_mode == "rows":
        x = x + _dot(o_ref[...].astype(BF16), wo_ref[...])
    x = _swiglu_half(x, g2_ref[...], win_ref, wout_ref)
    xn = _rms(x, g3_ref[...]).astype(BF16)
    gate = jax.nn.sigmoid(_dot(xn, wg_ref[...]))
    y = x + gate * _dot(p_ref[...].astype(BF16), wp_ref[...])
    if final_norm:
        y = _rms(y, fg_ref[...])
    y_ref[...] = y


def _post(x, o, w_o, o_mode, g2, w_in, w_out, p_all, layer, g3, w_gate, w_proj, final_g, final_norm, tm):
    b, t, d = x.shape
    dp = p_all.shape[-1]
    blk = pl.BlockSpec((None, tm, d), lambda i, j: (i, j, 0))
    args, specs = [x], [blk]
    if o_mode == "transposed":
        nh = o.shape[1]
        args += [o, w_o[0]]
        specs += [pl.BlockSpec((None, nh, V_DIM, tm), lambda i, j: (i, 0, 0, j)), _wspec(w_o)]
    elif o_mode == "rows":
        args += [o, w_o[0]]
        specs += [blk, _wspec(w_o)]
    args += [g2.reshape(1, d), w_in[0], w_out[0], p_all, g3.reshape(1, d), w_gate[0], w_proj[0],
             final_g.reshape(1, d)]
    specs += [_const_spec((1, d)), _wspec(w_in), _wspec(w_out),
              pl.BlockSpec((None, None, tm, dp), lambda i, j: (layer, i, j, 0)),
              _const_spec((1, d)), _wspec(w_gate), _wspec(w_proj),
              _const_spec((1, d))]
    return pl.pallas_call(
        functools.partial(_post_kernel, o_mode=o_mode, final_norm=final_norm),
        out_shape=jax.ShapeDtypeStruct((b, t, d), F32),
        grid=(b, t // tm),
        in_specs=specs,
        out_specs=blk,
        compiler_params=_params("parallel", "parallel"),
        name="post",
    )(*args)


LRU_SEQS = 8


def _softplus(z):
    return jnp.maximum(z, 0.0) + jnp.log1p(jnp.exp(-jnp.abs(z)))


def _lru_kernel(x_ref, conv0_ref, h0_ref, g_ref, win_ref, cw_ref, cb_ref, gaw_ref, gab_ref,
                gxw_ref, gxb_ref, lam_ref, wout_ref,
                o_ref, buf_ref, hlast_ref,
                ubuf_ref, a_ref, b_ref, hs_ref, hcar_ref, *, tt, d_rnn):
    t_id = pl.program_id(1)
    ns = LRU_SEQS
    tail = CONV_W - 1
    rows = tt * ns

    @pl.when(t_id == 0)
    def _():
        for k in range(tail):
            ubuf_ref[k * ns:(k + 1) * ns, :] = conv0_ref[:, k, :]
        hcar_ref[...] = h0_ref[...]

    x = jnp.swapaxes(x_ref[...], 0, 1).reshape(rows, x_ref.shape[2])
    xn = _rms(x, g_ref[...]).astype(BF16)
    yu = _dot(xn, win_ref[...])
    y = jax.nn.gelu(yu[:, :d_rnn])
    u = yu[:, d_rnn:]
    ubuf_ref[tail * ns:tail * ns + rows, :] = u

    conv = cb_ref[...] + ubuf_ref[0:rows, :] * cw_ref[0:1, :]
    conv = conv + ubuf_ref[ns:ns + rows, :] * cw_ref[1:2, :]
    conv = conv + ubuf_ref[2 * ns:2 * ns + rows, :] * cw_ref[2:3, :]
    conv = conv + u * cw_ref[3:4, :]

    bw = d_rnn // LRU_BLOCKS
    decay = _softplus(-lam_ref[...]) * (-LRU_C * LOG2E)
    for n in range(LRU_BLOCKS):
        sl = slice(n * bw, (n + 1) * bw)
        cn = conv[:, sl]
        cnb = cn.astype(BF16)
        r = jax.nn.sigmoid(_dot(cnb, gaw_ref[n]) + gab_ref[:, sl])
        i = jax.nn.sigmoid(_dot(cnb, gxw_ref[n]) + gxb_ref[:, sl])
        a = jnp.exp2(r * decay[:, sl])
        a_ref[:, sl] = a
        om = 1.0 - a * a
        mult = jnp.where(om > 0.0, om * lax.rsqrt(om), 0.0)
        b_ref[:, sl] = mult * (i * cn)

    h = hcar_ref[...]
    for t in range(tt):
        h = a_ref[t * ns:(t + 1) * ns, :] * h + b_ref[t * ns:(t + 1) * ns, :]
        hs_ref[t * ns:(t + 1) * ns, :] = h
    hcar_ref[...] = h

    z = (y * hs_ref[...]).astype(BF16)
    out = x + _dot(z, wout_ref[...])
    o_ref[...] = jnp.swapaxes(out.reshape(tt, ns, out.shape[1]), 0, 1)

    for k in range(tail):
        ubuf_ref[k * ns:(k + 1) * ns, :] = ubuf_ref[rows + k * ns:rows + (k + 1) * ns, :]

    @pl.when(t_id == pl.num_programs(1) - 1)
    def _():
        for k in range(tail):
            buf_ref[:, k, :] = ubuf_ref[rows + k * ns:rows + (k + 1) * ns, :]
        hlast_ref[...] = h


def _lru(x, conv0, h0, g, w_in, cw, cb, gaw, gab, gxw, gxb, lam, w_out, tt):
    b, t, d = x.shape
    d_rnn = w_out[0].shape[1]
    tail = CONV_W - 1
    ns = LRU_SEQS
    assert b % ns == 0 and t % tt == 0 and tt >= tail
    row = lambda v: v.reshape(1, -1)
    rows = tt * ns
    return pl.pallas_call(
        functools.partial(_lru_kernel, tt=tt, d_rnn=d_rnn),
        out_shape=(
            jax.ShapeDtypeStruct((b, t, d), F32),
            jax.ShapeDtypeStruct((b, tail, d_rnn), F32),
            jax.ShapeDtypeStruct((b, d_rnn), F32),
        ),
        grid=(b // ns, t // tt),
        in_specs=[
            pl.BlockSpec((ns, tt, d), lambda i, j: (i, j, 0)),
            pl.BlockSpec((ns, tail, d_rnn), lambda i, j: (i, 0, 0)),
            pl.BlockSpec((ns, d_rnn), lambda i, j: (i, 0)),
            _const_spec((1, d)),
            _wspec(w_in),
            _const_spec(cw.shape),
            _const_spec((1, d_rnn)),
            _wspec(gaw),
            _const_spec((1, d_rnn)),
            _wspec(gxw),
            _const_spec((1, d_rnn)),
            _const_spec((1, d_rnn)),
            _wspec(w_out),
        ],
        out_specs=(
            pl.BlockSpec((ns, tt, d), lambda i, j: (i, j, 0)),
            pl.BlockSpec((ns, tail, d_rnn), lambda i, j: (i, 0, 0)),
            pl.BlockSpec((ns, d_rnn), lambda i, j: (i, 0)),
        ),
        scratch_shapes=[
            pltpu.VMEM((rows + tail * ns, d_rnn), F32),
            pltpu.VMEM((rows, d_rnn), F32),
            pltpu.VMEM((rows, d_rnn), F32),
            pltpu.VMEM((rows, d_rnn), F32),
            pltpu.VMEM((ns, d_rnn), F32),
        ],
        compiler_params=_params("parallel", "arbitrary"),
        name="lru",
    )(x, conv0, h0, row(g), w_in[0], cw, row(cb), gaw[0], row(gab), gxw[0], row(gxb), row(lam),
      w_out[0])


Q_SCALE = HEAD_DIM ** -0.5 * LOG2E


def _key_position_features(n_rows):
    r = lax.broadcasted_iota(jnp.int32, (n_rows, V_DIM), 0) % KEY_BLOCK
    lane = lax.broadcasted_iota(jnp.int32, (n_rows, V_DIM), 1)
    digit = jnp.where(lane < 2, r // POS_RADIX, r % POS_RADIX)
    return jnp.where(lane < 4, digit, 0).astype(F32).astype(BF16)


def _qkv_kernel(x_ref, g_ref, w_ref, q_ref, k_ref, v_ref, *rest, d):
    xn = _rms(x_ref[...], g_ref[...]).astype(BF16)
    q = _dot(xn, w_ref[:, 0:d]) * Q_SCALE
    k = _dot(xn, w_ref[:, d:2 * d])
    v = _dot(xn, w_ref[:, 2 * d:3 * d])
    q_ref[...] = q.astype(q_ref.dtype)
    k_ref[...] = k
    v_ref[...] = v
    if rest:
        ka_ref, vt_ref = rest
        n_heads, n_kb = vt_ref.shape[0], vt_ref.shape[1]
        feat = _key_position_features(k.shape[0])
        pad_rows = lax.broadcasted_iota(jnp.int32, (VT_ROWS - V_DIM, KEY_BLOCK), 0)
        ones_row = jnp.where(pad_rows == 0, 1.0, 0.0).astype(BF16)
        for hh in range(n_heads):
            ka_ref[:, hh * K_AUG:hh * K_AUG + V_DIM] = k[:, hh * V_DIM:(hh + 1) * V_DIM].astype(BF16)
            ka_ref[:, hh * K_AUG + V_DIM:(hh + 1) * K_AUG] = feat
            for jb in range(n_kb):
                blk = v[jb * KEY_BLOCK:(jb + 1) * KEY_BLOCK, hh * V_DIM:(hh + 1) * V_DIM]
                vt_ref[hh, jb, :V_DIM, :] = blk.T.astype(BF16)
                vt_ref[hh, jb, V_DIM:, :] = ones_row


def _qkv_prompt(x, g, w, tm):
    b, t, d = x.shape
    nh = d // V_DIM
    assert tm % KEY_BLOCK == 0
    blk = pl.BlockSpec((None, tm, d), lambda i, j: (i, j, 0))
    kablk = pl.BlockSpec((None, tm, nh * K_AUG), lambda i, j: (i, j, 0))
    vtblk = pl.BlockSpec((None, nh, tm // KEY_BLOCK, VT_ROWS, KEY_BLOCK), lambda i, j: (i, 0, j, 0, 0))
    full = lambda dt: jax.ShapeDtypeStruct((b, t, d), dt)
    return pl.pallas_call(
        functools.partial(_qkv_kernel, d=d),
        out_shape=(full(BF16), full(F32), full(F32),
                   jax.ShapeDtypeStruct((b, t, nh * K_AUG), BF16),
                   jax.ShapeDtypeStruct((b, nh, t // KEY_BLOCK, VT_ROWS, KEY_BLOCK), BF16)),
        grid=(b, t // tm),
        in_specs=[blk, _const_spec((1, d)), _wspec(w)],
        out_specs=(blk, blk, blk, kablk, vtblk),
        compiler_params=_params("parallel", "parallel"),
        name="qkv_prompt",
    )(x, g.reshape(1, d), w[0])


def _qkv_sample(x, g, w, tm):
    n, d = x.shape
    blk = pl.BlockSpec((tm, d), lambda i: (i, 0))
    return pl.pallas_call(
        functools.partial(_qkv_kernel, d=d),
        out_shape=tuple([jax.ShapeDtypeStruct((n, d), F32)] * 3),
        grid=(n // tm,),
        in_specs=[blk, _const_spec((1, d)), _wspec(w)],
        out_specs=(blk, blk, blk),
        compiler_params=_params("parallel"),
        name="qkv_sample",
    )(x, g.reshape(1, d), w[0])


def _diff_lambda(lq1_ref, lk1_ref, lq2_ref, lk2_ref, lam_init):
    e1 = jnp.exp(jnp.sum(lq1_ref[...] * lk1_ref[...], axis=-1, keepdims=True))
    e2 = jnp.exp(jnp.sum(lq2_ref[...] * lk2_ref[...], axis=-1, keepdims=True))
    return e1 - e2 + lam_init


def _subln(o, sg, lam_init):
    return _rms(o, sg) * (1.0 - lam_init)


def _pattn_kernel(slopes_ref, q_ref, k_ref, vt_ref, lq1_ref, lk1_ref, lq2_ref, lk2_ref, sg_ref,
                  o_ref, qz_ref, st_ref, m_ref, acc_ref, *, tq, lam_init):
    h = pl.program_id(1)
    qi = pl.program_id(2)
    n_str = tq // Q_STRIPE
    slope = slopes_ref[h] * LOG2E

    q = q_ref[...]
    lane = lax.broadcasted_iota(jnp.int32, (1, 2 * HEAD_DIM), 1)
    zero = jnp.zeros_like(q)
    s_vec = jnp.full((1, V_DIM), slope, F32)
    s_hi = s_vec.astype(BF16).astype(F32)
    s_lo = (s_vec - s_hi).astype(BF16).astype(F32)
    w = jnp.where(lane == 0, s_hi * POS_RADIX, jnp.where(lane == 1, s_lo * POS_RADIX,
                  jnp.where(lane == 2, s_hi, jnp.where(lane == 3, s_lo, 0.0)))).astype(BF16)
    wq = jnp.broadcast_to(w, q.shape)
    qz_ref[0, :, :V_DIM] = jnp.where(lane < HEAD_DIM, q, zero)
    qz_ref[1, :, :V_DIM] = jnp.where(lane >= HEAD_DIM, q, zero)
    qz_ref[0, :, V_DIM:] = wq
    qz_ref[1, :, V_DIM:] = wq
    kloc = lax.broadcasted_iota(jnp.int32, (KEY_BLOCK, Q_STRIPE), 0)
    qloc = lax.broadcasted_iota(jnp.int32, (KEY_BLOCK, Q_STRIPE), 1)
    m_ref[...] = jnp.full(m_ref.shape, NEG_INF, F32)
    acc_ref[...] = jnp.zeros(acc_ref.shape, F32)

    def scores(c, s, kb):
        sl = slice(s * Q_STRIPE, (s + 1) * Q_STRIPE)
        koff = pl.multiple_of(kb * KEY_BLOCK, KEY_BLOCK)
        k = k_ref[pl.ds(koff, KEY_BLOCK), :]
        return lax.dot_general(k, qz_ref[c, sl, :], _NT, preferred_element_type=F32)

    def update(st, c, s, kb, masked):
        sl = slice(s * Q_STRIPE, (s + 1) * Q_STRIPE)
        if masked:
            st = jnp.where(kloc <= qloc, st, NEG_INF)
        coff = slope * (kb * KEY_BLOCK - (qi * tq + s * Q_STRIPE)).astype(F32)
        m_old = m_ref[c, :, sl]
        m_new = jnp.maximum(m_old, jnp.max(st, axis=0, keepdims=True) + coff)
        alpha = jnp.exp2(m_old - m_new)
        p = jnp.exp2(st - (m_new - coff))
        acc_ref[c, :, sl] = alpha * acc_ref[c, :, sl] + _dot(vt_ref[kb], p.astype(BF16))
        m_ref[c, :, sl] = m_new

    chains = [(c, s) for s in range(n_str) for c in range(2)]
    n_full = qi * n_str

    for i, (c, s) in enumerate(chains):
        st_ref[0, i] = scores(c, s, 0)

    def body(kk, carry):
        for slot in range(2):
            kb = 2 * kk + slot
            for i, (c, s) in enumerate(chains):
                st_ref[1 - slot, i] = scores(c, s, kb + 1)
                update(st_ref[slot, i], c, s, kb, False)
        return carry

    lax.fori_loop(0, n_full // 2, body, 0)

    for j in range(n_str):
        for i, (c, s) in enumerate(chains):
            if s >= j + 1 and j + 1 < n_str:
                st_ref[(j + 1) % 2, i] = scores(c, s, n_full + j + 1)
            if s >= j:
                update(st_ref[j % 2, i], c, s, n_full + j, s == j)

    lam = _diff_lambda(lq1_ref, lk1_ref, lq2_ref, lk2_ref, lam_init)
    inv1 = 1.0 / acc_ref[0, V_DIM:V_DIM + 1, :]
    inv2 = lam * (1.0 / acc_ref[1, V_DIM:V_DIM + 1, :])
    o = acc_ref[0, :V_DIM, :] * inv1 - acc_ref[1, :V_DIM, :] * inv2
    rs = lax.rsqrt(jnp.mean(o * o, axis=0, keepdims=True) + NORM_EPS)
    o_ref[...] = (o * rs * (sg_ref[...] * (1.0 - lam_init))).astype(o_ref.dtype)


def _pattn(q, k, vt, slopes, lq1, lk1, lq2, lk2, sg, lam_init, tq):
    b, t, d = q.shape
    nh = d // V_DIM
    n_str = tq // Q_STRIPE
    assert tq % Q_STRIPE == 0 and n_str % 2 == 0 and KEY_BLOCK == Q_STRIPE
    row = lambda a: a.reshape(1, -1)
    qblk = pl.BlockSpec((None, tq, V_DIM), lambda bi, hi, qi: (bi, qi, hi))
    kblk = pl.BlockSpec((None, t, K_AUG), lambda bi, hi, qi: (bi, 0, hi))
    vblk = pl.BlockSpec((None, None, t // KEY_BLOCK, VT_ROWS, KEY_BLOCK),
                        lambda bi, hi, qi: (bi, hi, 0, 0, 0))
    small = _const_spec((1, HEAD_DIM))
    return pl.pallas_call(
        functools.partial(_pattn_kernel, tq=tq, lam_init=lam_init),
        out_shape=jax.ShapeDtypeStruct((b, nh, V_DIM, t), BF16),
        grid=(b, nh, t // tq),
        in_specs=[pl.BlockSpec(memory_space=pltpu.SMEM), qblk, kblk, vblk,
                  small, small, small, small, _const_spec((V_DIM, 1))],
        out_specs=pl.BlockSpec((None, None, V_DIM, tq), lambda bi, hi, qi: (bi, hi, 0, qi)),
        scratch_shapes=[
            pltpu.VMEM((2, tq, K_AUG), BF16),
            pltpu.VMEM((2, 2 * n_str, KEY_BLOCK, Q_STRIPE), F32),
            pltpu.VMEM((2, 1, tq), F32),
            pltpu.VMEM((2, VT_ROWS, tq), F32),
        ],
        compiler_params=_params("parallel", "parallel", "arbitrary"),
        name="prompt_attn",
    )(slopes, q, k, vt, row(lq1), row(lk1), row(lq2), row(lk2), sg.reshape(-1, 1))


def _sattn_kernel(pt_ref, q_ref, kn_ref, vn_ref, *refs, n_heads, t_q, page, past, lam_init):
    del pt_ref
    npg = PAGES_PER_STEP
    ck_refs, cv_refs = refs[:npg], refs[npg:2 * npg]
    (lq1_ref, lk1_ref, lq2_ref, lk2_ref, sg_ref, o_ref,
     qm_ref, b0_ref, m_ref, l_ref, acc_ref) = refs[2 * npg:]
    p = pl.program_id(1)
    group = 2 * t_q
    rows = n_heads * group
    rid = lax.broadcasted_iota(jnp.int32, (rows, 1), 0)
    hrow = rid // group
    slope = jnp.exp2(-(hrow + 1).astype(F32)) * LOG2E

    @pl.when(p == 0)
    def _():
        q = q_ref[...].astype(BF16)
        lane = lax.broadcasted_iota(jnp.int32, (1, 2 * HEAD_DIM), 1)
        for hh in range(n_heads):
            qh = q[:, hh * V_DIM:(hh + 1) * V_DIM]
            zero = jnp.zeros_like(qh)
            qm_ref[hh * group:hh * group + t_q, :] = jnp.where(lane < HEAD_DIM, qh, zero)
            qm_ref[hh * group + t_q:(hh + 1) * group, :] = jnp.where(lane >= HEAD_DIM, qh, zero)
        cid = lax.broadcasted_iota(jnp.int32, (1, npg * page), 1)
        b0_ref[...] = slope * cid.astype(F32)

        nid = lax.broadcasted_iota(jnp.int32, (1, t_q * n_heads), 1)
        valid = (nid % n_heads == hrow) & (nid // n_heads <= rid % t_q)
        s = lax.dot_general(qm_ref[...], kn_ref[...].astype(BF16), _NT, preferred_element_type=F32)
        s = jnp.where(valid, s + slope * (nid // n_heads).astype(F32), NEG_INF)
        m0 = jnp.max(s, axis=1, keepdims=True)
        pe = jnp.exp2(s - m0)
        m_ref[...] = m0
        l_ref[...] = jnp.sum(pe, axis=1, keepdims=True)
        acc_ref[...] = _dot(pe.astype(BF16), vn_ref[...].astype(BF16))

    s_pages = []
    for j in range(npg):
        s_heads = []
        for hh in range(n_heads):
            kt = ck_refs[j][pl.ds(hh, page, stride=n_heads), :].T.astype(BF16)
            s_heads.append(_dot(qm_ref[hh * group:(hh + 1) * group, :], kt))
        s_pages.append(jnp.concatenate(s_heads, axis=0))
    s = jnp.concatenate(s_pages, axis=1) + b0_ref[...]
    c_p = slope * (p * (npg * page) - past).astype(F32)
    m_old = m_ref[...]
    m_new = jnp.maximum(m_old, jnp.max(s, axis=1, keepdims=True) + c_p)
    alpha = jnp.exp2(m_old - m_new)
    pe = jnp.exp2(s - (m_new - c_p))
    l_ref[...] = alpha * l_ref[...] + jnp.sum(pe, axis=1, keepdims=True)
    peb = pe.astype(BF16)
    pv_heads = []
    for hh in range(n_heads):
        pv = None
        for j in range(npg):
            vh = cv_refs[j][pl.ds(hh, page, stride=n_heads), :].astype(BF16)
            d = _dot(peb[hh * group:(hh + 1) * group, j * page:(j + 1) * page], vh)
            pv = d if pv is None else pv + d
        pv_heads.append(pv)
    acc_ref[...] = alpha * acc_ref[...] + jnp.concatenate(pv_heads, axis=0)
    m_ref[...] = m_new

    @pl.when(p == pl.num_programs(1) - 1)
    def _():
        lam = _diff_lambda(lq1_ref, lk1_ref, lq2_ref, lk2_ref, lam_init)
        o = acc_ref[...] / l_ref[...]
        for hh in range(n_heads):
            o1 = o[hh * group:hh * group + t_q, :]
            o2 = o[hh * group + t_q:(hh + 1) * group, :]
            on = _subln(o1 - lam * o2, sg_ref[...], lam_init)
            o_ref[:, hh * V_DIM:(hh + 1) * V_DIM] = on.astype(o_ref.dtype)


def _sattn(q, k_new, v_new, cache_k, cache_v, layer, page_table, lq1, lk1, lq2, lk2, sg, lam_init):
    b, t_q, d = q.shape
    n_pages = page_table.shape[1]
    n_layers, n_phys, page, n_heads, _ = cache_k.shape
    npg = PAGES_PER_STEP
    assert n_pages % npg == 0
    rows = n_heads * 2 * t_q
    row = lambda a: a.reshape(1, -1)
    ck = cache_k.reshape(n_layers, n_phys, page * n_heads, V_DIM)
    cv = cache_v.reshape(n_layers, n_phys, page * n_heads, V_DIM)
    small = pl.BlockSpec((1, HEAD_DIM), lambda bi, pi, pt: (0, 0))
    newblk = pl.BlockSpec((None, t_q * n_heads, V_DIM), lambda bi, pi, pt: (bi, 0, 0))

    def cblk(j):
        return pl.BlockSpec((None, None, page * n_heads, V_DIM),
                            lambda bi, pi, pt: (layer, pt[bi, pi * npg + j], 0, 0))

    qblk = pl.BlockSpec((None, t_q, d), lambda bi, pi, pt: (bi, 0, 0))
    grid_spec = pltpu.PrefetchScalarGridSpec(
        num_scalar_prefetch=1,
        grid=(b, n_pages // npg),
        in_specs=[qblk, newblk, newblk] + [cblk(j) for j in range(npg)] * 2
        + [small, small, small, small, pl.BlockSpec((1, V_DIM), lambda bi, pi, pt: (0, 0))],
        out_specs=qblk,
        scratch_shapes=[
            pltpu.VMEM((rows, V_DIM), BF16),
            pltpu.VMEM((rows, npg * page), F32),
            pltpu.VMEM((rows, 1), F32),
            pltpu.VMEM((rows, 1), F32),
            pltpu.VMEM((rows, V_DIM), F32),
        ],
    )
    return pl.pallas_call(
        functools.partial(_sattn_kernel, n_heads=n_heads, t_q=t_q, page=page,
                          past=n_pages * page, lam_init=lam_init),
        out_shape=jax.ShapeDtypeStruct((b, t_q, d), F32),
        grid_spec=grid_spec,
        compiler_params=_params("parallel", "arbitrary"),
        name="sample_attn",
    )(page_table, q, k_new.reshape(b, t_q * n_heads, V_DIM), v_new.reshape(b, t_q * n_heads, V_DIM),
      *([ck] * npg), *([cv] * npg), row(lq1), row(lk1), row(lq2), row(lk2), row(sg))


def _tile(n, pref):
    return pref if n % pref == 0 else n


def kernel(x_prompt, x_sample, cache_k, cache_v, page_table, state_conv, state_h, p_prompt, p_sample, norm_g, final_norm_g, ffn1_w_in, ffn1_w_out, ffn2_w_in, ffn2_w_out, ple_w_gate, ple_w_proj, lru_w_in, lru_conv_w, lru_conv_b, lru_gate_a_w, lru_gate_a_b, lru_gate_x_w, lru_gate_x_b, lru_lambda, lru_w_out, attn_w_qkv, attn_lambda_q1, attn_lambda_k1, attn_lambda_q2, attn_lambda_k2, attn_subln_g, attn_w_o):
    bp, tp, d = x_prompt.shape
    bs, ts, _ = x_sample.shape
    depth = norm_g.shape[0]
    n_heads = cache_k.shape[3]
    d_rnn = lru_w_out.shape[1]
    np_, ns = bp * tp, bs * ts
    tm_p, tm_s = _tile(np_, 512), _tile(ns, 256)
    tt_p = _tile(tp, 512)
    w1i, w1o = ffn1_w_in.astype(BF16), ffn1_w_out.astype(BF16)
    w2i, w2o = ffn2_w_in.astype(BF16), ffn2_w_out.astype(BF16)
    wpg, wpp = ple_w_gate.astype(BF16), ple_w_proj.astype(BF16)
    wli, wlo = lru_w_in.astype(BF16), lru_w_out.astype(BF16)
    wga, wgx = lru_gate_a_w.astype(BF16), lru_gate_x_w.astype(BF16)
    wqkv, wo = attn_w_qkv.astype(BF16), attn_w_o.astype(BF16)

    xp = x_prompt.reshape(np_, d)
    xs = x_sample.reshape(ns, d)
    ps_all = p_sample.reshape(depth, 1, ns, -1)
    slopes = jnp.asarray(2.0 ** (-8.0 * (jnp.arange(n_heads) + 1) / n_heads), F32)
    conv_zero = jnp.zeros((bp, CONV_W - 1, d_rnn), F32)
    h_zero = jnp.zeros((bp, d_rnn), F32)

    kp_l, vp_l, ks_l, vs_l = [], [], [], []
    cp_l, hp_l, cs_l, hs_l = [], [], [], []
    for i in range(depth):
        j = i // 2
        last = i == depth - 1
        xp = _ffn(xp, norm_g[i, 0], (w1i, i), (w1o, i), tm_p)
        xs = _ffn(xs, norm_g[i, 0], (w1i, i), (w1o, i), tm_s)
        post_w = (norm_g[i, 2], (w2i, i), (w2o, i))
        ple_w = (norm_g[i, 3], (wpg, i), (wpp, i), final_norm_g, last)
        if i % 2 == 0:
            lw = (norm_g[i, 1], (wli, j), lru_conv_w[j], lru_conv_b[j], (wga, j), lru_gate_a_b[j],
                  (wgx, j), lru_gate_x_b[j], lru_lambda[j], (wlo, j))
            xp3, buf_p, h_p = _lru(xp.reshape(bp, tp, d), conv_zero, h_zero, *lw, tt=_tile(tp, 64))
            xs3, buf_s, h_s = _lru(xs.reshape(bs, ts, d), state_conv[j], state_h[j], *lw, tt=ts)
            cp_l.append(buf_p); hp_l.append(h_p); cs_l.append(buf_s); hs_l.append(h_s)
            xp3 = _post(xp3, None, None, None, *post_w, p_prompt, i, *ple_w, tm=tt_p)
            xs3 = _post(xs3.reshape(1, ns, d), None, None, None, *post_w, ps_all, i, *ple_w, tm=tm_s)
        else:
            lam_init = 0.8 - 0.6 * math.exp(-0.3 * i)
            lams = (attn_lambda_q1[j], attn_lambda_k1[j], attn_lambda_q2[j], attn_lambda_k2[j],
                    attn_subln_g[j])
            xp3 = xp.reshape(bp, tp, d)
            q_p, k_p, v_p, ka_p, vt_p = _qkv_prompt(xp3, norm_g[i, 1], (wqkv, j), tt_p)
            q_s, k_s, v_s = _qkv_sample(xs, norm_g[i, 1], (wqkv, j), tm_s)
            o_p = _pattn(q_p, ka_p, vt_p, slopes, *lams, lam_init=lam_init, tq=_tile(tp, 2048))
            o_s = _sattn(q_s.reshape(bs, ts, d), k_s, v_s, cache_k, cache_v, j, page_table,
                         *lams, lam_init=lam_init)
            xp3 = _post(xp3, o_p, (wo, j), "transposed", *post_w, p_prompt, i, *ple_w, tm=tt_p)
            xs3 = _post(xs.reshape(1, ns, d), o_s.reshape(1, ns, d), (wo, j), "rows", *post_w, ps_all,
                        i, *ple_w, tm=tm_s)
            kp_l.append(k_p.reshape(bp, tp, n_heads, 2 * HEAD_DIM))
            vp_l.append(v_p.reshape(bp, tp, n_heads, V_DIM))
            ks_l.append(k_s.reshape(bs, ts, n_heads, 2 * HEAD_DIM))
            vs_l.append(v_s.reshape(bs, ts, n_heads, V_DIM))
        xp, xs = xp3.reshape(np_, d), xs3.reshape(ns, d)

    return (xp.reshape(bp, tp, d), xs.reshape(bs, ts, d),
            jnp.stack(kp_l), jnp.stack(vp_l), jnp.stack(ks_l), jnp.stack(vs_l),
            jnp.stack(cp_l), jnp.stack(hp_l), jnp.stack(cs_l), jnp.stack(hs_l))
```

```python
import functools
import math

import jax
import jax.numpy as jnp
from jax import lax
from jax.experimental import pallas as pl
from jax.experimental.pallas import tpu as pltpu

F32 = jnp.float32
BF16 = jnp.bfloat16

NORM_EPS = 1e-6
NEG_INF = -1e30
LRU_C = 8.0
CONV_W = 4
LRU_BLOCKS = 4
HEAD_DIM = 64
V_DIM = 128
LOG2E = math.log2(math.e)
KEY_BLOCK = 256
Q_STRIPE = 256
K_AUG = 2 * V_DIM
VT_ROWS = V_DIM + 16
POS_RADIX = 16
PAGES_PER_STEP = 16
FFN_CHUNK = 256
VMEM_LIMIT_BYTES = 56 * 1024 * 1024

_NT = (((1,), (1,)), ((), ()))
_TN = (((0,), (0,)), ((), ()))


def _params(*sem):
    return pltpu.CompilerParams(dimension_semantics=sem, vmem_limit_bytes=VMEM_LIMIT_BYTES)


def _const_spec(shape):
    nd = len(shape)
    return pl.BlockSpec(shape, lambda *_: (0,) * nd, pipeline_mode=pl.Buffered(1))


def _layer_spec(w_all, layer):
    nd = w_all.ndim
    return pl.BlockSpec((None,) + w_all.shape[1:], lambda *_: (layer,) + (0,) * (nd - 1),
                        pipeline_mode=pl.Buffered(1))


def _wspec(w):
    return _layer_spec(*w)


def _rms(x, g):
    y = x * lax.rsqrt(jnp.mean(x * x, axis=-1, keepdims=True) + NORM_EPS)
    return y * g


def _dot(a, b):
    return jnp.dot(a, b, preferred_element_type=F32)


def _swiglu_half(x, g, win_ref, wout_ref):
    d_ff = wout_ref.shape[0]
    xn = _rms(x, g).astype(BF16)
    acc = jnp.zeros(x.shape, F32)
    for c in range(d_ff // FFN_CHUNK):
        lo = c * FFN_CHUNK
        gt = _dot(xn, win_ref[:, lo:lo + FFN_CHUNK])
        ut = _dot(xn, win_ref[:, d_ff + lo:d_ff + lo + FFN_CHUNK])
        act = (gt * jax.nn.sigmoid(gt) * ut).astype(BF16)
        acc = acc + _dot(act, wout_ref[lo:lo + FFN_CHUNK, :])
    return x + 0.5 * acc


def _ffn_kernel(x_ref, g_ref, win_ref, wout_ref, o_ref):
    o_ref[...] = _swiglu_half(x_ref[...], g_ref[...], win_ref, wout_ref)


def _ffn(x, g, w_in, w_out, tm):
    n, d = x.shape
    return pl.pallas_call(
        _ffn_kernel,
        out_shape=jax.ShapeDtypeStruct((n, d), F32),
        grid=(n // tm,),
        in_specs=[
            pl.BlockSpec((tm, d), lambda i: (i, 0)),
            _const_spec((1, d)),
            _wspec(w_in),
            _wspec(w_out),
        ],
        out_specs=pl.BlockSpec((tm, d), lambda i: (i, 0)),
        compiler_params=_params("parallel"),
        name="ffn",
    )(x, g.reshape(1, d), w_in[0], w_out[0])


def _post_kernel(*refs, o_mode, final_norm):
    x_ref, refs = refs[0], refs[1:]
    if o_mode is not None:
        o_ref, wo_ref, refs = refs[0], refs[1], refs[2:]
    g2_ref, win_ref, wout_ref, p_ref, g3_ref, wg_ref, wp_ref, fg_ref, y_ref = refs
    x = x_ref[...]
    if o_mode == "transposed":
        nh, dv, tm = o_ref.shape
        x = x + lax.dot_general(o_ref[...].reshape(nh * dv, tm), wo_ref[...], _TN,
                                preferred_element_type=F32)
    elif o_mode == "rows":
        x = x + _dot(o_ref[...].astype(BF16), wo_ref[...])
    x = _swiglu_half(x, g2_ref[...], win_ref, wout_ref)
    xn = _rms(x, g3_ref[...]).astype(BF16)
    gate = jax.nn.sigmoid(_dot(xn, wg_ref[...]))
    y = x + gate * _dot(p_ref[...].astype(BF16), wp_ref[...])
    if final_norm:
        y = _rms(y, fg_ref[...])
    y_ref[...] = y


def _post(x, o, w_o, o_mode, g2, w_in, w_out, p_all, layer, g3, w_gate, w_proj, final_g, final_norm, tm):
    b, t, d = x.shape
    dp = p_all.shape[-1]
    blk = pl.BlockSpec((None, tm, d), lambda i, j: (i, j, 0))
    args, specs = [x], [blk]
    if o_mode == "transposed":
        nh = o.shape[1]
        args += [o, w_o[0]]
        specs += [pl.BlockSpec((None, nh, V_DIM, tm), lambda i, j: (i, 0, 0, j)), _wspec(w_o)]
    elif o_mode == "rows":
        args += [o, w_o[0]]
        specs += [blk, _wspec(w_o)]
    args += [g2.reshape(1, d), w_in[0], w_out[0], p_all, g3.reshape(1, d), w_gate[0], w_proj[0],
             final_g.reshape(1, d)]
    specs += [_const_spec((1, d)), _wspec(w_in), _wspec(w_out),
              pl.BlockSpec((None, None, tm, dp), lambda i, j: (layer, i, j, 0)),
              _const_spec((1, d)), _wspec(w_gate), _wspec(w_proj),
              _const_spec((1, d))]
    return pl.pallas_call(
        functools.partial(_post_kernel, o_mode=o_mode, final_norm=final_norm),
        out_shape=jax.ShapeDtypeStruct((b, t, d), F32),
        grid=(b, t // tm),
        in_specs=specs,
        out_specs=blk,
        compiler_params=_params("parallel", "parallel"),
        name="post",
    )(*args)


LRU_SEQS = 8


def _softplus(z):
    return jnp.maximum(z, 0.0) + jnp.log1p(jnp.exp(-jnp.abs(z)))


def _lru_kernel(x_ref, conv0_ref, h0_ref, g_ref, win_ref, cw_ref, cb_ref, gaw_ref, gab_ref,
                gxw_ref, gxb_ref, lam_ref, wout_ref,
                o_ref, buf_ref, hlast_ref,
                ubuf_ref, a_ref, b_ref, hs_ref, hcar_ref, *, tt, d_rnn):
    t_id = pl.program_id(1)
    ns = LRU_SEQS
    tail = CONV_W - 1
    rows = tt * ns

    @pl.when(t_id == 0)
    def _():
        for k in range(tail):
            ubuf_ref[k * ns:(k + 1) * ns, :] = conv0_ref[:, k, :]
        hcar_ref[...] = h0_ref[...]

    x = jnp.swapaxes(x_ref[...], 0, 1).reshape(rows, x_ref.shape[2])
    xn = _rms(x, g_ref[...]).astype(BF16)
    yu = _dot(xn, win_ref[...])
    y = jax.nn.gelu(yu[:, :d_rnn])
    u = yu[:, d_rnn:]
    ubuf_ref[tail * ns:tail * ns + rows, :] = u

    conv = cb_ref[...] + ubuf_ref[0:rows, :] * cw_ref[0:1, :]
    conv = conv + ubuf_ref[ns:ns + rows, :] * cw_ref[1:2, :]
    conv = conv + ubuf_ref[2 * ns:2 * ns + rows, :] * cw_ref[2:3, :]
    conv = conv + u * cw_ref[3:4, :]

    bw = d_rnn // LRU_BLOCKS
    decay = _softplus(-lam_ref[...]) * (-LRU_C * LOG2E)
    for n in range(LRU_BLOCKS):
        sl = slice(n * bw, (n + 1) * bw)
        cn = conv[:, sl]
        cnb = cn.astype(BF16)
        r = jax.nn.sigmoid(_dot(cnb, gaw_ref[n]) + gab_ref[:, sl])
        i = jax.nn.sigmoid(_dot(cnb, gxw_ref[n]) + gxb_ref[:, sl])
        a = jnp.exp2(r * decay[:, sl])
        a_ref[:, sl] = a
        om = 1.0 - a * a
        mult = jnp.where(om > 0.0, om * lax.rsqrt(om), 0.0)
        b_ref[:, sl] = mult * (i * cn)

    h = hcar_ref[...]
    for t in range(tt):
        h = a_ref[t * ns:(t + 1) * ns, :] * h + b_ref[t * ns:(t + 1) * ns, :]
        hs_ref[t * ns:(t + 1) * ns, :] = h
    hcar_ref[...] = h

    z = (y * hs_ref[...]).astype(BF16)
    out = x + _dot(z, wout_ref[...])
    o_ref[...] = jnp.swapaxes(out.reshape(tt, ns, out.shape[1]), 0, 1)

    for k in range(tail):
        ubuf_ref[k * ns:(k + 1) * ns, :] = ubuf_ref[rows + k * ns:rows + (k + 1) * ns, :]

    @pl.when(t_id == pl.num_programs(1) - 1)
    def _():
        for k in range(tail):
            buf_ref[:, k, :] = ubuf_ref[rows + k * ns:rows + (k + 1) * ns, :]
        hlast_ref[...] = h


def _lru(x, conv0, h0, g, w_in, cw, cb, gaw, gab, gxw, gxb, lam, w_out, tt):
    b, t, d = x.shape
    d_rnn = w_out[0].shape[1]
    tail = CONV_W - 1
    ns = LRU_SEQS
    assert b % ns == 0 and t % tt == 0 and tt >= tail
    row = lambda v: v.reshape(1, -1)
    rows = tt * ns
    return pl.pallas_call(
        functools.partial(_lru_kernel, tt=tt, d_rnn=d_rnn),
        out_shape=(
            jax.ShapeDtypeStruct((b, t, d), F32),
            jax.ShapeDtypeStruct((b, tail, d_rnn), F32),
            jax.ShapeDtypeStruct((b, d_rnn), F32),
        ),
        grid=(b // ns, t // tt),
        in_specs=[
            pl.BlockSpec((ns, tt, d), lambda i, j: (i, j, 0)),
            pl.BlockSpec((ns, tail, d_rnn), lambda i, j: (i, 0, 0)),
            pl.BlockSpec((ns, d_rnn), lambda i, j: (i, 0)),
            _const_spec((1, d)),
            _wspec(w_in),
            _const_spec(cw.shape),
            _const_spec((1, d_rnn)),
            _wspec(gaw),
            _const_spec((1, d_rnn)),
            _wspec(gxw),
            _const_spec((1, d_rnn)),
            _const_spec((1, d_rnn)),
            _wspec(w_out),
        ],
        out_specs=(
            pl.BlockSpec((ns, tt, d), lambda i, j: (i, j, 0)),
            pl.BlockSpec((ns, tail, d_rnn), lambda i, j: (i, 0, 0)),
            pl.BlockSpec((ns, d_rnn), lambda i, j: (i, 0)),
        ),
        scratch_shapes=[
            pltpu.VMEM((rows + tail * ns, d_rnn), F32),
            pltpu.VMEM((rows, d_rnn), F32),
            pltpu.VMEM((rows, d_rnn), F32),
            pltpu.VMEM((rows, d_rnn), F32),
            pltpu.VMEM((ns, d_rnn), F32),
        ],
        compiler_params=_params("parallel", "arbitrary"),
        name="lru",
    )(x, conv0, h0, row(g), w_in[0], cw, row(cb), gaw[0], row(gab), gxw[0], row(gxb), row(lam),
      w_out[0])


Q_SCALE = HEAD_DIM ** -0.5 * LOG2E


def _key_position_features(n_rows):
    r = lax.broadcasted_iota(jnp.int32, (n_rows, V_DIM), 0) % KEY_BLOCK
    lane = lax.broadcasted_iota(jnp.int32, (n_rows, V_DIM), 1)
    digit = jnp.where(lane < 2, r // POS_RADIX, r % POS_RADIX)
    return jnp.where(lane < 4, digit, 0).astype(F32).astype(BF16)


def _qkv_kernel(x_ref, g_ref, w_ref, q_ref, k_ref, v_ref, *rest, d):
    xn = _rms(x_ref[...], g_ref[...]).astype(BF16)
    q = _dot(xn, w_ref[:, 0:d]) * Q_SCALE
    k = _dot(xn, w_ref[:, d:2 * d])
    v = _dot(xn, w_ref[:, 2 * d:3 * d])
    q_ref[...] = q.astype(q_ref.dtype)
    k_ref[...] = k
    v_ref[...] = v
    if rest:
        ka_ref, vt_ref = rest
        n_heads, n_kb = vt_ref.shape[0], vt_ref.shape[1]
        feat = _key_position_features(k.shape[0])
        pad_rows = lax.broadcasted_iota(jnp.int32, (VT_ROWS - V_DIM, KEY_BLOCK), 0)
        ones_row = jnp.where(pad_rows == 0, 1.0, 0.0).astype(BF16)
        for hh in range(n_heads):
            ka_ref[:, hh * K_AUG:hh * K_AUG + V_DIM] = k[:, hh * V_DIM:(hh + 1) * V_DIM].astype(BF16)
            ka_ref[:, hh * K_AUG + V_DIM:(hh + 1) * K_AUG] = feat
            for jb in range(n_kb):
                blk = v[jb * KEY_BLOCK:(jb + 1) * KEY_BLOCK, hh * V_DIM:(hh + 1) * V_DIM]
                vt_ref[hh, jb, :V_DIM, :] = blk.T.astype(BF16)
                vt_ref[hh, jb, V_DIM:, :] = ones_row


def _qkv_prompt(x, g, w, tm):
    b, t, d = x.shape
    nh = d // V_DIM
    assert tm % KEY_BLOCK == 0
    blk = pl.BlockSpec((None, tm, d), lambda i, j: (i, j, 0))
    kablk = pl.BlockSpec((None, tm, nh * K_AUG), lambda i, j: (i, j, 0))
    vtblk = pl.BlockSpec((None, nh, tm // KEY_BLOCK, VT_ROWS, KEY_BLOCK), lambda i, j: (i, 0, j, 0, 0))
    full = lambda dt: jax.ShapeDtypeStruct((b, t, d), dt)
    return pl.pallas_call(
        functools.partial(_qkv_kernel, d=d),
        out_shape=(full(BF16), full(F32), full(F32),
                   jax.ShapeDtypeStruct((b, t, nh * K_AUG), BF16),
                   jax.ShapeDtypeStruct((b, nh, t // KEY_BLOCK, VT_ROWS, KEY_BLOCK), BF16)),
        grid=(b, t // tm),
        in_specs=[blk, _const_spec((1, d)), _wspec(w)],
        out_specs=(blk, blk, blk, kablk, vtblk),
        compiler_params=_params("parallel", "parallel"),
        name="qkv_prompt",
    )(x, g.reshape(1, d), w[0])


def _qkv_sample(x, g, w, tm):
    n, d = x.shape
    blk = pl.BlockSpec((tm, d), lambda i: (i, 0))
    return pl.pallas_call(
        functools.partial(_qkv_kernel, d=d),
        out_shape=tuple([jax.ShapeDtypeStruct((n, d), F32)] * 3),
        grid=(n // tm,),
        in_specs=[blk, _const_spec((1, d)), _wspec(w)],
        out_specs=(blk, blk, blk),
        compiler_params=_params("parallel"),
        name="qkv_sample",
    )(x, g.reshape(1, d), w[0])


def _diff_lambda(lq1_ref, lk1_ref, lq2_ref, lk2_ref, lam_init):
    e1 = jnp.exp(jnp.sum(lq1_ref[...] * lk1_ref[...], axis=-1, keepdims=True))
    e2 = jnp.exp(jnp.sum(lq2_ref[...] * lk2_ref[...], axis=-1, keepdims=True))
    return e1 - e2 + lam_init


def _subln(o, sg, lam_init):
    return _rms(o, sg) * (1.0 - lam_init)


def _pattn_kernel(slopes_ref, q_ref, k_ref, vt_ref, lq1_ref, lk1_ref, lq2_ref, lk2_ref, sg_ref,
                  o_ref, qz_ref, st_ref, m_ref, acc_ref, *, tq, lam_init):
    h = pl.program_id(1)
    qi = pl.program_id(2)
    n_str = tq // Q_STRIPE
    slope = slopes_ref[h] * LOG2E

    q = q_ref[...]
    lane = lax.broadcasted_iota(jnp.int32, (1, 2 * HEAD_DIM), 1)
    zero = jnp.zeros_like(q)
    s_vec = jnp.full((1, V_DIM), slope, F32)
    s_hi = s_vec.astype(BF16).astype(F32)
    s_lo = (s_vec - s_hi).astype(BF16).astype(F32)
    w = jnp.where(lane == 0, s_hi * POS_RADIX, jnp.where(lane == 1, s_lo * POS_RADIX,
                  jnp.where(lane == 2, s_hi, jnp.where(lane == 3, s_lo, 0.0)))).astype(BF16)
    wq = jnp.broadcast_to(w, q.shape)
    qz_ref[0, :, :V_DIM] = jnp.where(lane < HEAD_DIM, q, zero)
    qz_ref[1, :, :V_DIM] = jnp.where(lane >= HEAD_DIM, q, zero)
    qz_ref[0, :, V_DIM:] = wq
    qz_ref[1, :, V_DIM:] = wq
    kloc = lax.broadcasted_iota(jnp.int32, (KEY_BLOCK, Q_STRIPE), 0)
    qloc = lax.broadcasted_iota(jnp.int32, (KEY_BLOCK, Q_STRIPE), 1)
    m_ref[...] = jnp.full(m_ref.shape, NEG_INF, F32)
    acc_ref[...] = jnp.zeros(acc_ref.shape, F32)

    def scores(c, s, kb):
        sl = slice(s * Q_STRIPE, (s + 1) * Q_STRIPE)
        koff = pl.multiple_of(kb * KEY_BLOCK, KEY_BLOCK)
        k = k_ref[pl.ds(koff, KEY_BLOCK), :]
        return lax.dot_general(k, qz_ref[c, sl, :], _NT, preferred_element_type=F32)

    def update(st, c, s, kb, masked):
        sl = slice(s * Q_STRIPE, (s + 1) * Q_STRIPE)
        if masked:
            st = jnp.where(kloc <= qloc, st, NEG_INF)
        coff = slope * (kb * KEY_BLOCK - (qi * tq + s * Q_STRIPE)).astype(F32)
        m_old = m_ref[c, :, sl]
        m_new = jnp.maximum(m_old, jnp.max(st, axis=0, keepdims=True) + coff)
        alpha = jnp.exp2(m_old - m_new)
        p = jnp.exp2(st - (m_new - coff))
        acc_ref[c, :, sl] = alpha * acc_ref[c, :, sl] + _dot(vt_ref[kb], p.astype(BF16))
        m_ref[c, :, sl] = m_new

    chains = [(c, s) for s in range(n_str) for c in range(2)]
    n_full = qi * n_str

    for i, (c, s) in enumerate(chains):
        st_ref[0, i] = scores(c, s, 0)

    def body(kk, carry):
        for slot in range(2):
            kb = 2 * kk + slot
            for i, (c, s) in enumerate(chains):
                st_ref[1 - slot, i] = scores(c, s, kb + 1)
                update(st_ref[slot, i], c, s, kb, False)
        return carry

    lax.fori_loop(0, n_full // 2, body, 0)

    for j in range(n_str):
        for i, (c, s) in enumerate(chains):
            if s >= j + 1 and j + 1 < n_str:
                st_ref[(j + 1) % 2, i] = scores(c, s, n_full + j + 1)
            if s >= j:
                update(st_ref[j % 2, i], c, s, n_full + j, s == j)

    lam = _diff_lambda(lq1_ref, lk1_ref, lq2_ref, lk2_ref, lam_init)
    inv1 = 1.0 / acc_ref[0, V_DIM:V_DIM + 1, :]
    inv2 = lam * (1.0 / acc_ref[1, V_DIM:V_DIM + 1, :])
    o = acc_ref[0, :V_DIM, :] * inv1 - acc_ref[1, :V_DIM, :] * inv2
    rs = lax.rsqrt(jnp.mean(o * o, axis=0, keepdims=True) + NORM_EPS)
    o_ref[...] = (o * rs * (sg_ref[...] * (1.0 - lam_init))).astype(o_ref.dtype)


def _pattn(q, k, vt, slopes, lq1, lk1, lq2, lk2, sg, lam_init, tq):
    b, t, d = q.shape
    nh = d // V_DIM
    n_str = tq // Q_STRIPE
    assert tq % Q_STRIPE == 0 and n_str % 2 == 0 and KEY_BLOCK == Q_STRIPE
    row = lambda a: a.reshape(1, -1)
    qblk = pl.BlockSpec((None, tq, V_DIM), lambda bi, hi, qi: (bi, qi, hi))
    kblk = pl.BlockSpec((None, t, K_AUG), lambda bi, hi, qi: (bi, 0, hi))
    vblk = pl.BlockSpec((None, None, t // KEY_BLOCK, VT_ROWS, KEY_BLOCK),
                        lambda bi, hi, qi: (bi, hi, 0, 0, 0))
    small = _const_spec((1, HEAD_DIM))
    return pl.pallas_call(
        functools.partial(_pattn_kernel, tq=tq, lam_init=lam_init),
        out_shape=jax.ShapeDtypeStruct((b, nh, V_DIM, t), BF16),
        grid=(b, nh, t // tq),
        in_specs=[pl.BlockSpec(memory_space=pltpu.SMEM), qblk, kblk, vblk,
                  small, small, small, small, _const_spec((V_DIM, 1))],
        out_specs=pl.BlockSpec((None, None, V_DIM, tq), lambda bi, hi, qi: (bi, hi, 0, qi)),
        scratch_shapes=[
            pltpu.VMEM((2, tq, K_AUG), BF16),
            pltpu.VMEM((2, 2 * n_str, KEY_BLOCK, Q_STRIPE), F32),
            pltpu.VMEM((2, 1, tq), F32),
            pltpu.VMEM((2, VT_ROWS, tq), F32),
        ],
        compiler_params=_params("parallel", "parallel", "arbitrary"),
        name="prompt_attn",
    )(slopes, q, k, vt, row(lq1), row(lk1), row(lq2), row(lk2), sg.reshape(-1, 1))


def _sattn_kernel(pt_ref, q_ref, kn_ref, vn_ref, *refs, n_heads, t_q, page, past, lam_init):
    del pt_ref
    npg = PAGES_PER_STEP
    ck_refs, cv_refs = refs[:npg], refs[npg:2 * npg]
    (lq1_ref, lk1_ref, lq2_ref, lk2_ref, sg_ref, o_ref,
     qm_ref, b0_ref, m_ref, l_ref, acc_ref) = refs[2 * npg:]
    p = pl.program_id(1)
    group = 2 * t_q
    rows = n_heads * group
    rid = lax.broadcasted_iota(jnp.int32, (rows, 1), 0)
    hrow = rid // group
    slope = jnp.exp2(-(hrow + 1).astype(F32)) * LOG2E

    @pl.when(p == 0)
    def _():
        q = q_ref[...].astype(BF16)
        lane = lax.broadcasted_iota(jnp.int32, (1, 2 * HEAD_DIM), 1)
        for hh in range(n_heads):
            qh = q[:, hh * V_DIM:(hh + 1) * V_DIM]
            zero = jnp.zeros_like(qh)
            qm_ref[hh * group:hh * group + t_q, :] = jnp.where(lane < HEAD_DIM, qh, zero)
            qm_ref[hh * group + t_q:(hh + 1) * group, :] = jnp.where(lane >= HEAD_DIM, qh, zero)
        cid = lax.broadcasted_iota(jnp.int32, (1, npg * page), 1)
        b0_ref[...] = slope * cid.astype(F32)

        nid = lax.broadcasted_iota(jnp.int32, (1, t_q * n_heads), 1)
        valid = (nid % n_heads == hrow) & (nid // n_heads <= rid % t_q)
        s = lax.dot_general(qm_ref[...], kn_ref[...].astype(BF16), _NT, preferred_element_type=F32)
        s = jnp.where(valid, s + slope * (nid // n_heads).astype(F32), NEG_INF)
        m0 = jnp.max(s, axis=1, keepdims=True)
        pe = jnp.exp2(s - m0)
        m_ref[...] = m0
        l_ref[...] = jnp.sum(pe, axis=1, keepdims=True)
        acc_ref[...] = _dot(pe.astype(BF16), vn_ref[...].astype(BF16))

    s_pages = []
    for j in range(npg):
        s_heads = []
        for hh in range(n_heads):
            kt = ck_refs[j][pl.ds(hh, page, stride=n_heads), :].T.astype(BF16)
            s_heads.append(_dot(qm_ref[hh * group:(hh + 1) * group, :], kt))
        s_pages.append(jnp.concatenate(s_heads, axis=0))
    s = jnp.concatenate(s_pages, axis=1) + b0_ref[...]
    c_p = slope * (p * (npg * page) - past).astype(F32)
    m_old = m_ref[...]
    m_new = jnp.maximum(m_old, jnp.max(s, axis=1, keepdims=True) + c_p)
    alpha = jnp.exp2(m_old - m_new)
    pe = jnp.exp2(s - (m_new - c_p))
    l_ref[...] = alpha * l_ref[...] + jnp.sum(pe, axis=1, keepdims=True)
    peb = pe.astype(BF16)
    pv_heads = []
    for hh in range(n_heads):
        pv = None
        for j in range(npg):
            vh = cv_refs[j][pl.ds(hh, page, stride=n_heads), :].astype(BF16)
            d = _dot(peb[hh * group:(hh + 1) * group, j * page:(j + 1) * page], vh)
            pv = d if pv is None else pv + d
        pv_heads.append(pv)
    acc_ref[...] = alpha * acc_ref[...] + jnp.concatenate(pv_heads, axis=0)
    m_ref[...] = m_new

    @pl.when(p == pl.num_programs(1) - 1)
    def _():
        lam = _diff_lambda(lq1_ref, lk1_ref, lq2_ref, lk2_ref, lam_init)
        o = acc_ref[...] / l_ref[...]
        for hh in range(n_heads):
            o1 = o[hh * group:hh * group + t_q, :]
            o2 = o[hh * group + t_q:(hh + 1) * group, :]
            on = _subln(o1 - lam * o2, sg_ref[...], lam_init)
            o_ref[:, hh * V_DIM:(hh + 1) * V_DIM] = on.astype(o_ref.dtype)


def _sattn(q, k_new, v_new, cache_k, cache_v, layer, page_table, lq1, lk1, lq2, lk2, sg, lam_init):
    b, t_q, d = q.shape
    n_pages = page_table.shape[1]
    n_layers, n_phys, page, n_heads, _ = cache_k.shape
    npg = PAGES_PER_STEP
    assert n_pages % npg == 0
    rows = n_heads * 2 * t_q
    row = lambda a: a.reshape(1, -1)
    ck = cache_k.reshape(n_layers, n_phys, page * n_heads, V_DIM)
    cv = cache_v.reshape(n_layers, n_phys, page * n_heads, V_DIM)
    small = pl.BlockSpec((1, HEAD_DIM), lambda bi, pi, pt: (0, 0))
    newblk = pl.BlockSpec((None, t_q * n_heads, V_DIM), lambda bi, pi, pt: (bi, 0, 0))

    def cblk(j):
        return pl.BlockSpec((None, None, page * n_heads, V_DIM),
                            lambda bi, pi, pt: (layer, pt[bi, pi * npg + j], 0, 0))

    qblk = pl.BlockSpec((None, t_q, d), lambda bi, pi, pt: (bi, 0, 0))
    grid_spec = pltpu.PrefetchScalarGridSpec(
        num_scalar_prefetch=1,
        grid=(b, n_pages // npg),
        in_specs=[qblk, newblk, newblk] + [cblk(j) for j in range(npg)] * 2
        + [small, small, small, small, pl.BlockSpec((1, V_DIM), lambda bi, pi, pt: (0, 0))],
        out_specs=qblk,
        scratch_shapes=[
            pltpu.VMEM((rows, V_DIM), BF16),
            pltpu.VMEM((rows, npg * page), F32),
            pltpu.VMEM((rows, 1), F32),
            pltpu.VMEM((rows, 1), F32),
            pltpu.VMEM((rows, V_DIM), F32),
        ],
    )
    return pl.pallas_call(
        functools.partial(_sattn_kernel, n_heads=n_heads, t_q=t_q, page=page,
                          past=n_pages * page, lam_init=lam_init),
        out_shape=jax.ShapeDtypeStruct((b, t_q, d), F32),
        grid_spec=grid_spec,
        compiler_params=_params("parallel", "arbitrary"),
        name="sample_attn",
    )(page_table, q, k_new.reshape(b, t_q * n_heads, V_DIM), v_new.reshape(b, t_q * n_heads, V_DIM),
      *([ck] * npg), *([cv] * npg), row(lq1), row(lk1), row(lq2), row(lk2), row(sg))


def _tile(n, pref):
    return pref if n % pref == 0 else n


def kernel(x_prompt, x_sample, cache_k, cache_v, page_table, state_conv, state_h, p_prompt, p_sample, norm_g, final_norm_g, ffn1_w_in, ffn1_w_out, ffn2_w_in, ffn2_w_out, ple_w_gate, ple_w_proj, lru_w_in, lru_conv_w, lru_conv_b, lru_gate_a_w, lru_gate_a_b, lru_gate_x_w, lru_gate_x_b, lru_lambda, lru_w_out, attn_w_qkv, attn_lambda_q1, attn_lambda_k1, attn_lambda_q2, attn_lambda_k2, attn_subln_g, attn_w_o):
    bp, tp, d = x_prompt.shape
    bs, ts, _ = x_sample.shape
    depth = norm_g.shape[0]
    n_heads = cache_k.shape[3]
    d_rnn = lru_w_out.shape[1]
    np_, ns = bp * tp, bs * ts
    tm_p, tm_s = _tile(np_, 1024), _tile(ns, 256)
    tt_p = _tile(tp, 512)
    w1i, w1o = ffn1_w_in.astype(BF16), ffn1_w_out.astype(BF16)
    w2i, w2o = ffn2_w_in.astype(BF16), ffn2_w_out.astype(BF16)
    wpg, wpp = ple_w_gate.astype(BF16), ple_w_proj.astype(BF16)
    wli, wlo = lru_w_in.astype(BF16), lru_w_out.astype(BF16)
    wga, wgx = lru_gate_a_w.astype(BF16), lru_gate_x_w.astype(BF16)
    wqkv, wo = attn_w_qkv.astype(BF16), attn_w_o.astype(BF16)

    xp = x_prompt.reshape(np_, d)
    xs = x_sample.reshape(ns, d)
    ps_all = p_sample.reshape(depth, 1, ns, -1)
    slopes = jnp.asarray(2.0 ** (-8.0 * (jnp.arange(n_heads) + 1) / n_heads), F32)
    conv_zero = jnp.zeros((bp, CONV_W - 1, d_rnn), F32)
    h_zero = jnp.zeros((bp, d_rnn), F32)

    kp_l, vp_l, ks_l, vs_l = [], [], [], []
    cp_l, hp_l, cs_l, hs_l = [], [], [], []
    for i in range(depth):
        j = i // 2
        last = i == depth - 1
        xp = _ffn(xp, norm_g[i, 0], (w1i, i), (w1o, i), tm_p)
        xs = _ffn(xs, norm_g[i, 0], (w1i, i), (w1o, i), tm_s)
        post_w = (norm_g[i, 2], (w2i, i), (w2o, i))
        ple_w = (norm_g[i, 3], (wpg, i), (wpp, i), final_norm_g, last)
        if i % 2 == 0:
            lw = (norm_g[i, 1], (wli, j), lru_conv_w[j], lru_conv_b[j], (wga, j), lru_gate_a_b[j],
                  (wgx, j), lru_gate_x_b[j], lru_lambda[j], (wlo, j))
            xp3, buf_p, h_p = _lru(xp.reshape(bp, tp, d), conv_zero, h_zero, *lw, tt=_tile(tp, 64))
            xs3, buf_s, h_s = _lru(xs.reshape(bs, ts, d), state_conv[j], state_h[j], *lw, tt=ts)
            cp_l.append(buf_p); hp_l.append(h_p); cs_l.append(buf_s); hs_l.append(h_s)
            xp3 = _post(xp3, None, None, None, *post_w, p_prompt, i, *ple_w, tm=tt_p)
            xs3 = _post(xs3.reshape(1, ns, d), None, None, None, *post_w, ps_all, i, *ple_w, tm=tm_s)
        else:
            lam_init = 0.8 - 0.6 * math.exp(-0.3 * i)
            lams = (attn_lambda_q1[j], attn_lambda_k1[j], attn_lambda_q2[j], attn_lambda_k2[j],
                    attn_subln_g[j])
            xp3 = xp.reshape(bp, tp, d)
            q_p, k_p, v_p, ka_p, vt_p = _qkv_prompt(xp3, norm_g[i, 1], (wqkv, j), tt_p)
            q_s, k_s, v_s = _qkv_sample(xs, norm_g[i, 1], (wqkv, j), tm_s)
            o_p = _pattn(q_p, ka_p, vt_p, slopes, *lams, lam_init=lam_init, tq=_tile(tp, 2048))
            o_s = _sattn(q_s.reshape(bs, ts, d), k_s, v_s, cache_k, cache_v, j, page_table,
                         *lams, lam_init=lam_init)
            xp3 = _post(xp3, o_p, (wo, j), "transposed", *post_w, p_prompt, i, *ple_w, tm=tt_p)
            xs3 = _post(xs.reshape(1, ns, d), o_s.reshape(1, ns, d), (wo, j), "rows", *post_w, ps_all,
                        i, *ple_w, tm=tm_s)
            kp_l.append(k_p.reshape(bp, tp, n_heads, 2 * HEAD_DIM))
            vp_l.append(v_p.reshape(bp, tp, n_heads, V_DIM))
            ks_l.append(k_s.reshape(bs, ts, n_heads, 2 * HEAD_DIM))
            vs_l.append(v_s.reshape(bs, ts, n_heads, V_DIM))
        xp, xs = xp3.reshape(np_, d), xs3.reshape(ns, d)

    return (xp.reshape(bp, tp, d), xs.reshape(bs, ts, d),
            jnp.stack(kp_l), jnp.stack(vp_l), jnp.stack(ks_l), jnp.stack(vs_l),
            jnp.stack(cp_l), jnp.stack(hp_l), jnp.stack(cs_l), jnp.stack(hs_l))
```

```python
import functools
import math

import jax
import jax.numpy as jnp
from jax import lax
from jax.experimental import pallas as pl
from jax.experimental.pallas import tpu as pltpu

F32 = jnp.float32
BF16 = jnp.bfloat16

NORM_EPS = 1e-6
NEG_INF = -1e30
LRU_C = 8.0
CONV_W = 4
LRU_BLOCKS = 4
HEAD_DIM = 64
V_DIM = 128
LOG2E = math.log2(math.e)
KEY_BLOCK = 256
Q_STRIPE = 256
K_AUG = 2 * V_DIM
VT_ROWS = V_DIM + 16
POS_RADIX = 16
PAGES_PER_STEP = 16
FFN_CHUNK = 256
VMEM_LIMIT_BYTES = 56 * 1024 * 1024

_NT = (((1,), (1,)), ((), ()))
_TN = (((0,), (0,)), ((), ()))


def _params(*sem):
    return pltpu.CompilerParams(dimension_semantics=sem, vmem_limit_bytes=VMEM_LIMIT_BYTES)


def _const_spec(shape):
    nd = len(shape)
    return pl.BlockSpec(shape, lambda *_: (0,) * nd, pipeline_mode=pl.Buffered(1))


def _layer_spec(w_all, layer):
    nd = w_all.ndim
    return pl.BlockSpec((None,) + w_all.shape[1:], lambda *_: (layer,) + (0,) * (nd - 1),
                        pipeline_mode=pl.Buffered(1))


def _wspec(w):
    return _layer_spec(*w)


def _rms(x, g):
    y = x * lax.rsqrt(jnp.mean(x * x, axis=-1, keepdims=True) + NORM_EPS)
    return y * g


def _dot(a, b):
    return jnp.dot(a, b, preferred_element_type=F32)


def _swiglu_half(x, g, win_ref, wout_ref):
    d_ff = wout_ref.shape[0]
    xn = _rms(x, g).astype(BF16)
    acc = jnp.zeros(x.shape, F32)
    for c in range(d_ff // FFN_CHUNK):
        lo = c * FFN_CHUNK
        gt = _dot(xn, win_ref[:, lo:lo + FFN_CHUNK])
        ut = _dot(xn, win_ref[:, d_ff + lo:d_ff + lo + FFN_CHUNK])
        act = (gt * jax.nn.sigmoid(gt) * ut).astype(BF16)
        acc = acc + _dot(act, wout_ref[lo:lo + FFN_CHUNK, :])
    return x + 0.5 * acc


def _ffn_kernel(x_ref, g_ref, win_ref, wout_ref, o_ref):
    o_ref[...] = _swiglu_half(x_ref[...], g_ref[...], win_ref, wout_ref)


def _ffn(x, g, w_in, w_out, tm):
    n, d = x.shape
    return pl.pallas_call(
        _ffn_kernel,
        out_shape=jax.ShapeDtypeStruct((n, d), F32),
        grid=(n // tm,),
        in_specs=[
            pl.BlockSpec((tm, d), lambda i: (i, 0)),
            _const_spec((1, d)),
            _wspec(w_in),
            _wspec(w_out),
        ],
        out_specs=pl.BlockSpec((tm, d), lambda i: (i, 0)),
        compiler_params=_params("parallel"),
        name="ffn",
    )(x, g.reshape(1, d), w_in[0], w_out[0])


def _post_kernel(*refs, o_mode, final_norm):
    x_ref, refs = refs[0], refs[1:]
    if o_mode is not None:
        o_ref, wo_ref, refs = refs[0], refs[1], refs[2:]
    g2_ref, win_ref, wout_ref, p_ref, g3_ref, wg_ref, wp_ref, fg_ref, y_ref = refs
    x = x_ref[...]
    if o_mode == "transposed":
        nh, dv, tm = o_ref.shape
        x = x + lax.dot_general(o_ref[...].reshape(nh * dv, tm), wo_ref[...], _TN,
                                preferred_element_type=F32)
    elif o_mode == "rows":
        x = x + _dot(o_ref[...].astype(BF16), wo_ref[...])
    x = _swiglu_half(x, g2_ref[...], win_ref, wout_ref)
    xn = _rms(x, g3_ref[...]).astype(BF16)
    gate = jax.nn.sigmoid(_dot(xn, wg_ref[...]))
    y = x + gate * _dot(p_ref[...].astype(BF16), wp_ref[...])
    if final_norm:
        y = _rms(y, fg_ref[...])
    y_ref[...] = y


def _post(x, o, w_o, o_mode, g2, w_in, w_out, p_all, layer, g3, w_gate, w_proj, final_g, final_norm, tm):
    b, t, d = x.shape
    dp = p_all.shape[-1]
    blk = pl.BlockSpec((None, tm, d), lambda i, j: (i, j, 0))
    args, specs = [x], [blk]
    if o_mode == "transposed":
        nh = o.shape[1]
        args += [o, w_o[0]]
        specs += [pl.BlockSpec((None, nh, V_DIM, tm), lambda i, j: (i, 0, 0, j)), _wspec(w_o)]
    elif o_mode == "rows":
        args += [o, w_o[0]]
        specs += [blk, _wspec(w_o)]
    args += [g2.reshape(1, d), w_in[0], w_out[0], p_all, g3.reshape(1, d), w_gate[0], w_proj[0],
             final_g.reshape(1, d)]
    specs += [_const_spec((1, d)), _wspec(w_in), _wspec(w_out),
              pl.BlockSpec((None, None, tm, dp), lambda i, j: (layer, i, j, 0)),
              _const_spec((1, d)), _wspec(w_gate), _wspec(w_proj),
              _const_spec((1, d))]
    return pl.pallas_call(
        functools.partial(_post_kernel, o_mode=o_mode, final_norm=final_norm),
        out_shape=jax.ShapeDtypeStruct((b, t, d), F32),
        grid=(b, t // tm),
        in_specs=specs,
        out_specs=blk,
        compiler_params=_params("parallel", "parallel"),
        name="post",
    )(*args)


LRU_SEQS = 8


def _softplus(z):
    return jnp.maximum(z, 0.0) + jnp.log1p(jnp.exp(-jnp.abs(z)))


def _lru_kernel(x_ref, conv0_ref, h0_ref, g_ref, win_ref, cw_ref, cb_ref, gaw_ref, gab_ref,
                gxw_ref, gxb_ref, lam_ref, wout_ref,
                o_ref, buf_ref, hlast_ref,
                ubuf_ref, a_ref, b_ref, hs_ref, hcar_ref, *, tt, d_rnn):
    t_id = pl.program_id(1)
    ns = LRU_SEQS
    tail = CONV_W - 1
    rows = tt * ns

    @pl.when(t_id == 0)
    def _():
        for k in range(tail):
            ubuf_ref[k * ns:(k + 1) * ns, :] = conv0_ref[:, k, :]
        hcar_ref[...] = h0_ref[...]

    x = jnp.swapaxes(x_ref[...], 0, 1).reshape(rows, x_ref.shape[2])
    xn = _rms(x, g_ref[...]).astype(BF16)
    yu = _dot(xn, win_ref[...])
    y = jax.nn.gelu(yu[:, :d_rnn])
    u = yu[:, d_rnn:]
    ubuf_ref[tail * ns:tail * ns + rows, :] = u

    conv = cb_ref[...] + ubuf_ref[0:rows, :] * cw_ref[0:1, :]
    conv = conv + ubuf_ref[ns:ns + rows, :] * cw_ref[1:2, :]
    conv = conv + ubuf_ref[2 * ns:2 * ns + rows, :] * cw_ref[2:3, :]
    conv = conv + u * cw_ref[3:4, :]

    bw = d_rnn // LRU_BLOCKS
    decay = _softplus(-lam_ref[...]) * (-LRU_C * LOG2E)
    for n in range(LRU_BLOCKS):
        sl = slice(n * bw, (n + 1) * bw)
        cn = conv[:, sl]
        cnb = cn.astype(BF16)
        r = jax.nn.sigmoid(_dot(cnb, gaw_ref[n]) + gab_ref[:, sl])
        i = jax.nn.sigmoid(_dot(cnb, gxw_ref[n]) + gxb_ref[:, sl])
        a = jnp.exp2(r * decay[:, sl])
        a_ref[:, sl] = a
        om = 1.0 - a * a
        mult = jnp.where(om > 0.0, om * lax.rsqrt(om), 0.0)
        b_ref[:, sl] = mult * (i * cn)

    h = hcar_ref[...]
    for t in range(tt):
        h = a_ref[t * ns:(t + 1) * ns, :] * h + b_ref[t * ns:(t + 1) * ns, :]
        hs_ref[t * ns:(t + 1) * ns, :] = h
    hcar_ref[...] = h

    z = (y * hs_ref[...]).astype(BF16)
    out = x + _dot(z, wout_ref[...])
    o_ref[...] = jnp.swapaxes(out.reshape(tt, ns, out.shape[1]), 0, 1)

    for k in range(tail):
        ubuf_ref[k * ns:(k + 1) * ns, :] = ubuf_ref[rows + k * ns:rows + (k + 1) * ns, :]

    @pl.when(t_id == pl.num_programs(1) - 1)
    def _():
        for k in range(tail):
            buf_ref[:, k, :] = ubuf_ref[rows + k * ns:rows + (k + 1) * ns, :]
        hlast_ref[...] = h


def _lru(x, conv0, h0, g, w_in, cw, cb, gaw, gab, gxw, gxb, lam, w_out, tt):
    b, t, d = x.shape
    d_rnn = w_out[0].shape[1]
    tail = CONV_W - 1
    ns = LRU_SEQS
    assert b % ns == 0 and t % tt == 0 and tt >= tail
    row = lambda v: v.reshape(1, -1)
    rows = tt * ns
    return pl.pallas_call(
        functools.partial(_lru_kernel, tt=tt, d_rnn=d_rnn),
        out_shape=(
            jax.ShapeDtypeStruct((b, t, d), F32),
            jax.ShapeDtypeStruct((b, tail, d_rnn), F32),
            jax.ShapeDtypeStruct((b, d_rnn), F32),
        ),
        grid=(b // ns, t // tt),
        in_specs=[
            pl.BlockSpec((ns, tt, d), lambda i, j: (i, j, 0)),
            pl.BlockSpec((ns, tail, d_rnn), lambda i, j: (i, 0, 0)),
            pl.BlockSpec((ns, d_rnn), lambda i, j: (i, 0)),
            _const_spec((1, d)),
            _wspec(w_in),
            _const_spec(cw.shape),
            _const_spec((1, d_rnn)),
            _wspec(gaw),
            _const_spec((1, d_rnn)),
            _wspec(gxw),
            _const_spec((1, d_rnn)),
            _const_spec((1, d_rnn)),
            _wspec(w_out),
        ],
        out_specs=(
            pl.BlockSpec((ns, tt, d), lambda i, j: (i, j, 0)),
            pl.BlockSpec((ns, tail, d_rnn), lambda i, j: (i, 0, 0)),
            pl.BlockSpec((ns, d_rnn), lambda i, j: (i, 0)),
        ),
        scratch_shapes=[
            pltpu.VMEM((rows + tail * ns, d_rnn), F32),
            pltpu.VMEM((rows, d_rnn), F32),
            pltpu.VMEM((rows, d_rnn), F32),
            pltpu.VMEM((rows, d_rnn), F32),
            pltpu.VMEM((ns, d_rnn), F32),
        ],
        compiler_params=_params("parallel", "arbitrary"),
        name="lru",
    )(x, conv0, h0, row(g), w_in[0], cw, row(cb), gaw[0], row(gab), gxw[0], row(gxb), row(lam),
      w_out[0])


Q_SCALE = HEAD_DIM ** -0.5 * LOG2E


def _key_position_features(n_rows):
    r = lax.broadcasted_iota(jnp.int32, (n_rows, V_DIM), 0) % KEY_BLOCK
    lane = lax.broadcasted_iota(jnp.int32, (n_rows, V_DIM), 1)
    digit = jnp.where(lane < 2, r // POS_RADIX, r % POS_RADIX)
    return jnp.where(lane < 4, digit, 0).astype(F32).astype(BF16)


def _qkv_kernel(x_ref, g_ref, w_ref, q_ref, k_ref, v_ref, *rest, d):
    xn = _rms(x_ref[...], g_ref[...]).astype(BF16)
    q = _dot(xn, w_ref[:, 0:d]) * Q_SCALE
    k = _dot(xn, w_ref[:, d:2 * d])
    v = _dot(xn, w_ref[:, 2 * d:3 * d])
    q_ref[...] = q.astype(q_ref.dtype)
    k_ref[...] = k
    v_ref[...] = v
    if rest:
        kb_ref, vt_ref = rest
        kb_ref[...] = k.astype(BF16)
        n_heads, n_kb = vt_ref.shape[0], vt_ref.shape[1]
        pad_rows = lax.broadcasted_iota(jnp.int32, (VT_ROWS - V_DIM, KEY_BLOCK), 0)
        ones_row = jnp.where(pad_rows == 0, 1.0, 0.0).astype(BF16)
        for hh in range(n_heads):
            for jb in range(n_kb):
                blk = v[jb * KEY_BLOCK:(jb + 1) * KEY_BLOCK, hh * V_DIM:(hh + 1) * V_DIM]
                vt_ref[hh, jb, :V_DIM, :] = blk.T.astype(BF16)
                vt_ref[hh, jb, V_DIM:, :] = ones_row


def _qkv_prompt(x, g, w, tm):
    b, t, d = x.shape
    nh = d // V_DIM
    assert tm % KEY_BLOCK == 0
    blk = pl.BlockSpec((None, tm, d), lambda i, j: (i, j, 0))
    vtblk = pl.BlockSpec((None, nh, tm // KEY_BLOCK, VT_ROWS, KEY_BLOCK), lambda i, j: (i, 0, j, 0, 0))
    full = lambda dt: jax.ShapeDtypeStruct((b, t, d), dt)
    return pl.pallas_call(
        functools.partial(_qkv_kernel, d=d),
        out_shape=(full(BF16), full(F32), full(F32),
                   full(BF16),
                   jax.ShapeDtypeStruct((b, nh, t // KEY_BLOCK, VT_ROWS, KEY_BLOCK), BF16)),
        grid=(b, t // tm),
        in_specs=[blk, _const_spec((1, d)), _wspec(w)],
        out_specs=(blk, blk, blk, blk, vtblk),
        compiler_params=_params("parallel", "parallel"),
        name="qkv_prompt",
    )(x, g.reshape(1, d), w[0])


def _qkv_sample(x, g, w, tm):
    n, d = x.shape
    blk = pl.BlockSpec((tm, d), lambda i: (i, 0))
    return pl.pallas_call(
        functools.partial(_qkv_kernel, d=d),
        out_shape=tuple([jax.ShapeDtypeStruct((n, d), F32)] * 3),
        grid=(n // tm,),
        in_specs=[blk, _const_spec((1, d)), _wspec(w)],
        out_specs=(blk, blk, blk),
        compiler_params=_params("parallel"),
        name="qkv_sample",
    )(x, g.reshape(1, d), w[0])


def _diff_lambda(lq1_ref, lk1_ref, lq2_ref, lk2_ref, lam_init):
    e1 = jnp.exp(jnp.sum(lq1_ref[...] * lk1_ref[...], axis=-1, keepdims=True))
    e2 = jnp.exp(jnp.sum(lq2_ref[...] * lk2_ref[...], axis=-1, keepdims=True))
    return e1 - e2 + lam_init


def _subln(o, sg, lam_init):
    return _rms(o, sg) * (1.0 - lam_init)


def _pattn_kernel(slopes_ref, q_ref, k_ref, vt_ref, lq1_ref, lk1_ref, lq2_ref, lk2_ref, sg_ref,
                  o_ref, ka_ref, qz_ref, st_ref, m_ref, acc_ref, *, tq, lam_init):
    h = pl.program_id(1)
    qi = pl.program_id(2)
    n_str = tq // Q_STRIPE
    slope = slopes_ref[h] * LOG2E

    @pl.when(qi == 0)
    def _():
        feat = _key_position_features(KEY_BLOCK)
        for jb in range(k_ref.shape[0] // KEY_BLOCK):
            rows = slice(jb * KEY_BLOCK, (jb + 1) * KEY_BLOCK)
            ka_ref[rows, :V_DIM] = k_ref[rows, :]
            ka_ref[rows, V_DIM:] = feat

    q = q_ref[...]
    lane = lax.broadcasted_iota(jnp.int32, (1, 2 * HEAD_DIM), 1)
    zero = jnp.zeros_like(q)
    s_vec = jnp.full((1, V_DIM), slope, F32)
    s_hi = s_vec.astype(BF16).astype(F32)
    s_lo = (s_vec - s_hi).astype(BF16).astype(F32)
    w = jnp.where(lane == 0, s_hi * POS_RADIX, jnp.where(lane == 1, s_lo * POS_RADIX,
                  jnp.where(lane == 2, s_hi, jnp.where(lane == 3, s_lo, 0.0)))).astype(BF16)
    wq = jnp.broadcast_to(w, q.shape)
    qz_ref[0, :, :V_DIM] = jnp.where(lane < HEAD_DIM, q, zero)
    qz_ref[1, :, :V_DIM] = jnp.where(lane >= HEAD_DIM, q, zero)
    qz_ref[0, :, V_DIM:] = wq
    qz_ref[1, :, V_DIM:] = wq
    kloc = lax.broadcasted_iota(jnp.int32, (KEY_BLOCK, Q_STRIPE), 0)
    qloc = lax.broadcasted_iota(jnp.int32, (KEY_BLOCK, Q_STRIPE), 1)
    m_ref[...] = jnp.full(m_ref.shape, NEG_INF, F32)
    acc_ref[...] = jnp.zeros(acc_ref.shape, F32)

    def scores(c, s, kb):
        sl = slice(s * Q_STRIPE, (s + 1) * Q_STRIPE)
        koff = pl.multiple_of(kb * KEY_BLOCK, KEY_BLOCK)
        k = ka_ref[pl.ds(koff, KEY_BLOCK), :]
        return lax.dot_general(k, qz_ref[c, sl, :], _NT, preferred_element_type=F32)

    def update(st, c, s, kb, masked):
        sl = slice(s * Q_STRIPE, (s + 1) * Q_STRIPE)
        if masked:
            st = jnp.where(kloc <= qloc, st, NEG_INF)
        coff = slope * (kb * KEY_BLOCK - (qi * tq + s * Q_STRIPE)).astype(F32)
        m_old = m_ref[c, :, sl]
        m_new = jnp.maximum(m_old, jnp.max(st, axis=0, keepdims=True) + coff)
        alpha = jnp.exp2(m_old - m_new)
        p = jnp.exp2(st - (m_new - coff))
        acc_ref[c, :, sl] = alpha * acc_ref[c, :, sl] + _dot(vt_ref[kb], p.astype(BF16))
        m_ref[c, :, sl] = m_new

    chains = [(c, s) for s in range(n_str) for c in range(2)]
    n_full = qi * n_str

    for i, (c, s) in enumerate(chains):
        st_ref[0, i] = scores(c, s, 0)

    def body(kk, carry):
        for slot in range(2):
            kb = 2 * kk + slot
            for i, (c, s) in enumerate(chains):
                st_ref[1 - slot, i] = scores(c, s, kb + 1)
                update(st_ref[slot, i], c, s, kb, False)
        return carry

    lax.fori_loop(0, n_full // 2, body, 0)

    for j in range(n_str):
        for i, (c, s) in enumerate(chains):
            if s >= j + 1 and j + 1 < n_str:
                st_ref[(j + 1) % 2, i] = scores(c, s, n_full + j + 1)
            if s >= j:
                update(st_ref[j % 2, i], c, s, n_full + j, s == j)

    lam = _diff_lambda(lq1_ref, lk1_ref, lq2_ref, lk2_ref, lam_init)
    inv1 = 1.0 / acc_ref[0, V_DIM:V_DIM + 1, :]
    inv2 = lam * (1.0 / acc_ref[1, V_DIM:V_DIM + 1, :])
    o = acc_ref[0, :V_DIM, :] * inv1 - acc_ref[1, :V_DIM, :] * inv2
    rs = lax.rsqrt(jnp.mean(o * o, axis=0, keepdims=True) + NORM_EPS)
    o_ref[...] = (o * rs * (sg_ref[...] * (1.0 - lam_init))).astype(o_ref.dtype)


def _pattn(q, k, vt, slopes, lq1, lk1, lq2, lk2, sg, lam_init, tq):
    b, t, d = q.shape
    nh = d // V_DIM
    n_str = tq // Q_STRIPE
    assert tq % Q_STRIPE == 0 and n_str % 2 == 0 and KEY_BLOCK == Q_STRIPE
    row = lambda a: a.reshape(1, -1)
    qblk = pl.BlockSpec((None, tq, V_DIM), lambda bi, hi, qi: (bi, qi, hi))
    kblk = pl.BlockSpec((None, t, V_DIM), lambda bi, hi, qi: (bi, 0, hi))
    vblk = pl.BlockSpec((None, None, t // KEY_BLOCK, VT_ROWS, KEY_BLOCK),
                        lambda bi, hi, qi: (bi, hi, 0, 0, 0))
    small = _const_spec((1, HEAD_DIM))
    return pl.pallas_call(
        functools.partial(_pattn_kernel, tq=tq, lam_init=lam_init),
        out_shape=jax.ShapeDtypeStruct((b, nh, V_DIM, t), BF16),
        grid=(b, nh, t // tq),
        in_specs=[pl.BlockSpec(memory_space=pltpu.SMEM), qblk, kblk, vblk,
                  small, small, small, small, _const_spec((V_DIM, 1))],
        out_specs=pl.BlockSpec((None, None, V_DIM, tq), lambda bi, hi, qi: (bi, hi, 0, qi)),
        scratch_shapes=[
            pltpu.VMEM((t, K_AUG), BF16),
            pltpu.VMEM((2, tq, K_AUG), BF16),
            pltpu.VMEM((2, 2 * n_str, KEY_BLOCK, Q_STRIPE), F32),
            pltpu.VMEM((2, 1, tq), F32),
            pltpu.VMEM((2, VT_ROWS, tq), F32),
        ],
        compiler_params=_params("parallel", "parallel", "arbitrary"),
        name="prompt_attn",
    )(slopes, q, k, vt, row(lq1), row(lk1), row(lq2), row(lk2), sg.reshape(-1, 1))


def _sattn_kernel(pt_ref, q_ref, kn_ref, vn_ref, *refs, n_heads, t_q, page, past, lam_init):
    del pt_ref
    npg = PAGES_PER_STEP
    ck_refs, cv_refs = refs[:npg], refs[npg:2 * npg]
    (lq1_ref, lk1_ref, lq2_ref, lk2_ref, sg_ref, o_ref,
     qm_ref, b0_ref, m_ref, l_ref, acc_ref) = refs[2 * npg:]
    p = pl.program_id(1)
    group = 2 * t_q
    rows = n_heads * group
    rid = lax.broadcasted_iota(jnp.int32, (rows, 1), 0)
    hrow = rid // group
    slope = jnp.exp2(-(hrow + 1).astype(F32)) * LOG2E

    @pl.when(p == 0)
    def _():
        q = q_ref[...].astype(BF16)
        lane = lax.broadcasted_iota(jnp.int32, (1, 2 * HEAD_DIM), 1)
        for hh in range(n_heads):
            qh = q[:, hh * V_DIM:(hh + 1) * V_DIM]
            zero = jnp.zeros_like(qh)
            qm_ref[hh * group:hh * group + t_q, :] = jnp.where(lane < HEAD_DIM, qh, zero)
            qm_ref[hh * group + t_q:(hh + 1) * group, :] = jnp.where(lane >= HEAD_DIM, qh, zero)
        cid = lax.broadcasted_iota(jnp.int32, (1, npg * page), 1)
        b0_ref[...] = slope * cid.astype(F32)

        nid = lax.broadcasted_iota(jnp.int32, (1, t_q * n_heads), 1)
        valid = (nid % n_heads == hrow) & (nid // n_heads <= rid % t_q)
        s = lax.dot_general(qm_ref[...], kn_ref[...].astype(BF16), _NT, preferred_element_type=F32)
        s = jnp.where(valid, s + slope * (nid // n_heads).astype(F32), NEG_INF)
        m0 = jnp.max(s, axis=1, keepdims=True)
        pe = jnp.exp2(s - m0)
        m_ref[...] = m0
        l_ref[...] = jnp.sum(pe, axis=1, keepdims=True)
        acc_ref[...] = _dot(pe.astype(BF16), vn_ref[...].astype(BF16))

    s_pages = []
    for j in range(npg):
        s_heads = []
        for hh in range(n_heads):
            kt = ck_refs[j][pl.ds(hh, page, stride=n_heads), :].T.astype(BF16)
            s_heads.append(_dot(qm_ref[hh * group:(hh + 1) * group, :], kt))
        s_pages.append(jnp.concatenate(s_heads, axis=0))
    s = jnp.concatenate(s_pages, axis=1) + b0_ref[...]
    c_p = slope * (p * (npg * page) - past).astype(F32)
    m_old = m_ref[...]
    m_new = jnp.maximum(m_old, jnp.max(s, axis=1, keepdims=True) + c_p)
    alpha = jnp.exp2(m_old - m_new)
    pe = jnp.exp2(s - (m_new - c_p))
    l_ref[...] = alpha * l_ref[...] + jnp.sum(pe, axis=1, keepdims=True)
    peb = pe.astype(BF16)
    pv_heads = []
    for hh in range(n_heads):
        pv = None
        for j in range(npg):
            vh = cv_refs[j][pl.ds(hh, page, stride=n_heads), :].astype(BF16)
            d = _dot(peb[hh * group:(hh + 1) * group, j * page:(j + 1) * page], vh)
            pv = d if pv is None else pv + d
        pv_heads.append(pv)
    acc_ref[...] = alpha * acc_ref[...] + jnp.concatenate(pv_heads, axis=0)
    m_ref[...] = m_new

    @pl.when(p == pl.num_programs(1) - 1)
    def _():
        lam = _diff_lambda(lq1_ref, lk1_ref, lq2_ref, lk2_ref, lam_init)
        o = acc_ref[...] / l_ref[...]
        for hh in range(n_heads):
            o1 = o[hh * group:hh * group + t_q, :]
            o2 = o[hh * group + t_q:(hh + 1) * group, :]
            on = _subln(o1 - lam * o2, sg_ref[...], lam_init)
            o_ref[:, hh * V_DIM:(hh + 1) * V_DIM] = on.astype(o_ref.dtype)


def _sattn(q, k_new, v_new, cache_k, cache_v, layer, page_table, lq1, lk1, lq2, lk2, sg, lam_init):
    b, t_q, d = q.shape
    n_pages = page_table.shape[1]
    n_layers, n_phys, page, n_heads, _ = cache_k.shape
    npg = PAGES_PER_STEP
    assert n_pages % npg == 0
    rows = n_heads * 2 * t_q
    row = lambda a: a.reshape(1, -1)
    ck = cache_k.reshape(n_layers, n_phys, page * n_heads, V_DIM)
    cv = cache_v.reshape(n_layers, n_phys, page * n_heads, V_DIM)
    small = pl.BlockSpec((1, HEAD_DIM), lambda bi, pi, pt: (0, 0))
    newblk = pl.BlockSpec((None, t_q * n_heads, V_DIM), lambda bi, pi, pt: (bi, 0, 0))

    def cblk(j):
        return pl.BlockSpec((None, None, page * n_heads, V_DIM),
                            lambda bi, pi, pt: (layer, pt[bi, pi * npg + j], 0, 0))

    qblk = pl.BlockSpec((None, t_q, d), lambda bi, pi, pt: (bi, 0, 0))
    grid_spec = pltpu.PrefetchScalarGridSpec(
        num_scalar_prefetch=1,
        grid=(b, n_pages // npg),
        in_specs=[qblk, newblk, newblk] + [cblk(j) for j in range(npg)] * 2
        + [small, small, small, small, pl.BlockSpec((1, V_DIM), lambda bi, pi, pt: (0, 0))],
        out_specs=qblk,
        scratch_shapes=[
            pltpu.VMEM((rows, V_DIM), BF16),
            pltpu.VMEM((rows, npg * page), F32),
            pltpu.VMEM((rows, 1), F32),
            pltpu.VMEM((rows, 1), F32),
            pltpu.VMEM((rows, V_DIM), F32),
        ],
    )
    return pl.pallas_call(
        functools.partial(_sattn_kernel, n_heads=n_heads, t_q=t_q, page=page,
                          past=n_pages * page, lam_init=lam_init),
        out_shape=jax.ShapeDtypeStruct((b, t_q, d), F32),
        grid_spec=grid_spec,
        compiler_params=_params("parallel", "arbitrary"),
        name="sample_attn",
    )(page_table, q, k_new.reshape(b, t_q * n_heads, V_DIM), v_new.reshape(b, t_q * n_heads, V_DIM),
      *([ck] * npg), *([cv] * npg), row(lq1), row(lk1), row(lq2), row(lk2), row(sg))


def _tile(n, pref):
    return pref if n % pref == 0 else n


def kernel(x_prompt, x_sample, cache_k, cache_v, page_table, state_conv, state_h, p_prompt, p_sample, norm_g, final_norm_g, ffn1_w_in, ffn1_w_out, ffn2_w_in, ffn2_w_out, ple_w_gate, ple_w_proj, lru_w_in, lru_conv_w, lru_conv_b, lru_gate_a_w, lru_gate_a_b, lru_gate_x_w, lru_gate_x_b, lru_lambda, lru_w_out, attn_w_qkv, attn_lambda_q1, attn_lambda_k1, attn_lambda_q2, attn_lambda_k2, attn_subln_g, attn_w_o):
    bp, tp, d = x_prompt.shape
    bs, ts, _ = x_sample.shape
    depth = norm_g.shape[0]
    n_heads = cache_k.shape[3]
    d_rnn = lru_w_out.shape[1]
    np_, ns = bp * tp, bs * ts
    tm_p, tm_s = _tile(np_, 1024), _tile(ns, 256)
    tt_p = _tile(tp, 512)
    w1i, w1o = ffn1_w_in.astype(BF16), ffn1_w_out.astype(BF16)
    w2i, w2o = ffn2_w_in.astype(BF16), ffn2_w_out.astype(BF16)
    wpg, wpp = ple_w_gate.astype(BF16), ple_w_proj.astype(BF16)
    wli, wlo = lru_w_in.astype(BF16), lru_w_out.astype(BF16)
    wga, wgx = lru_gate_a_w.astype(BF16), lru_gate_x_w.astype(BF16)
    wqkv, wo = attn_w_qkv.astype(BF16), attn_w_o.astype(BF16)

    xp = x_prompt.reshape(np_, d)
    xs = x_sample.reshape(ns, d)
    ps_all = p_sample.reshape(depth, 1, ns, -1)
    slopes = jnp.asarray(2.0 ** (-8.0 * (jnp.arange(n_heads) + 1) / n_heads), F32)
    conv_zero = jnp.zeros((bp, CONV_W - 1, d_rnn), F32)
    h_zero = jnp.zeros((bp, d_rnn), F32)

    kp_l, vp_l, ks_l, vs_l = [], [], [], []
    cp_l, hp_l, cs_l, hs_l = [], [], [], []
    for i in range(depth):
        j = i // 2
        last = i == depth - 1
        xp = _ffn(xp, norm_g[i, 0], (w1i, i), (w1o, i), tm_p)
        xs = _ffn(xs, norm_g[i, 0], (w1i, i), (w1o, i), tm_s)
        post_w = (norm_g[i, 2], (w2i, i), (w2o, i))
        ple_w = (norm_g[i, 3], (wpg, i), (wpp, i), final_norm_g, last)
        if i % 2 == 0:
            lw = (norm_g[i, 1], (wli, j), lru_conv_w[j], lru_conv_b[j], (wga, j), lru_gate_a_b[j],
                  (wgx, j), lru_gate_x_b[j], lru_lambda[j], (wlo, j))
            xp3, buf_p, h_p = _lru(xp.reshape(bp, tp, d), conv_zero, h_zero, *lw, tt=_tile(tp, 64))
            xs3, buf_s, h_s = _lru(xs.reshape(bs, ts, d), state_conv[j], state_h[j], *lw, tt=ts)
            cp_l.append(buf_p); hp_l.append(h_p); cs_l.append(buf_s); hs_l.append(h_s)
            xp3 = _post(xp3, None, None, None, *post_w, p_prompt, i, *ple_w, tm=tt_p)
            xs3 = _post(xs3.reshape(1, ns, d), None, None, None, *post_w, ps_all, i, *ple_w, tm=tm_s)
        else:
            lam_init = 0.8 - 0.6 * math.exp(-0.3 * i)
            lams = (attn_lambda_q1[j], attn_lambda_k1[j], attn_lambda_q2[j], attn_lambda_k2[j],
                    attn_subln_g[j])
            xp3 = xp.reshape(bp, tp, d)
            q_p, k_p, v_p, ka_p, vt_p = _qkv_prompt(xp3, norm_g[i, 1], (wqkv, j), tt_p)
            q_s, k_s, v_s = _qkv_sample(xs, norm_g[i, 1], (wqkv, j), tm_s)
            o_p = _pattn(q_p, ka_p, vt_p, slopes, *lams, lam_init=lam_init, tq=_tile(tp, 2048))
            o_s = _sattn(q_s.reshape(bs, ts, d), k_s, v_s, cache_k, cache_v, j, page_table,
                         *lams, lam_init=lam_init)
            xp3 = _post(xp3, o_p, (wo, j), "transposed", *post_w, p_prompt, i, *ple_w, tm=tt_p)
            xs3 = _post(xs.reshape(1, ns, d), o_s.reshape(1, ns, d), (wo, j), "rows", *post_w, ps_all,
                        i, *ple_w, tm=tm_s)
            kp_l.append(k_p.reshape(bp, tp, n_heads, 2 * HEAD_DIM))
            vp_l.append(v_p.reshape(bp, tp, n_heads, V_DIM))
            ks_l.append(k_s.reshape(bs, ts, n_heads, 2 * HEAD_DIM))
            vs_l.append(v_s.reshape(bs, ts, n_heads, V_DIM))
        xp, xs = xp3.reshape(np_, d), xs3.reshape(ns, d)

    return (xp.reshape(bp, tp, d), xs.reshape(bs, ts, d),
            jnp.stack(kp_l), jnp.stack(vp_l), jnp.stack(ks_l), jnp.stack(vs_l),
            jnp.stack(cp_l), jnp.stack(hp_l), jnp.stack(cs_l), jnp.stack(hs_l))
```
